```python
import jax, jax.numpy as jnp
from jax import lax
import numpy as np

D_MODEL = 1024
BATCH = 2
SEQ = 8192
DEPTH = 2

D_A = D_MODEL
CONV_W = 3
D_B = D_MODEL
DH_B = 128
G_B = D_B // DH_B
CHUNK = 128
D_C = D_MODEL
DH_C = 64
H_C = D_C // DH_C
QBLK = 128

IN_AB = 4 * D_A + 3 * D_B
IN_C = 4 * D_C
N_EVEN = (DEPTH + 1) // 2
N_ODD = DEPTH // 2
EPS = 1e-6

kernel_name = "hybrid_conv_sgmlp_stickbreak_adaln"


def rms_norm(x, g):
    xf = x.astype(jnp.float32)
    y = xf * lax.rsqrt(jnp.mean(xf * xf, axis=-1, keepdims=True) + EPS)
    return (y * g.astype(jnp.float32)).astype(x.dtype)


def conv_sgmlp_mixer(h, w_in, conv_w, sg_norm, sg_w, sg_b, w_out):
    bsz, s, _ = h.shape
    proj = h @ w_in
    cuts = [D_A, 2 * D_A, 3 * D_A, 4 * D_A, 4 * D_A + D_B, 4 * D_A + 2 * D_B]
    a_b, a_c, a_x, a_z, b_u, b_v, b_z = jnp.split(proj, cuts, axis=-1)
    conv = lax.conv_general_dilated(
        a_c * a_x, conv_w[:, None, :].astype(h.dtype), window_strides=(1,),
        padding=[(CONV_W - 1, 0)], dimension_numbers=('NWC', 'WIO', 'NWC'),
        feature_group_count=D_A)
    y_a = a_b * conv * jax.nn.silu(a_z)
    nc = s // CHUNK
    v = rms_norm(b_v.reshape(bsz, nc, CHUNK, G_B, DH_B), sg_norm)
    mask = jnp.tril(jnp.ones((CHUNK, CHUNK), dtype=bool))
    w_s = jnp.where(mask[None], sg_w, 0.0)
    sgate = jnp.einsum('gts,bcsgd->bctgd', w_s, v) + sg_b.T[:, :, None]
    y_b = b_u * sgate.reshape(bsz, s, D_B) * jax.nn.silu(b_z)
    return jnp.concatenate([y_a, y_b], axis=-1) @ w_out


def stick_breaking_attention(q, k, v):
    bsz, s, h, dh = q.shape
    nb = s // QBLK
    qf = q.astype(jnp.float32) * (dh ** -0.5)
    kf = k.astype(jnp.float32)
    vf = v.astype(jnp.float32)
    q_blocks = qf.reshape(bsz, nb, QBLK, h, dh).transpose(1, 0, 2, 3, 4)
    key_pos = jnp.arange(s)

    def block(args):
        qb, i = args
        qpos = i * QBLK + jnp.arange(QBLK)
        z = jnp.einsum('bthd,bshd->bhts', qb, kf)
        causal = key_pos[None, :] < qpos[:, None]
        log_beta = jax.nn.log_sigmoid(z)
        log_1mb = jnp.where(causal, log_beta - z, 0.0)
        suffix = lax.cumsum(log_1mb, axis=3, reverse=True) - log_1mb
        w = jnp.where(causal, jnp.exp(log_beta + suffix), 0.0)
        return jnp.einsum('bhts,bshd->bthd', w, vf)

    out = lax.map(block, (q_blocks, jnp.arange(nb)))
    return out.transpose(1, 0, 2, 3, 4).reshape(bsz, s, h, dh).astype(q.dtype)


def stick_breaking_mixer(h, w_in, q_norm, k_norm, w_out):
    bsz, s, _ = h.shape
    q, k, v, z = jnp.split(h @ w_in, 4, axis=-1)
    q = rms_norm(q.reshape(bsz, s, H_C, DH_C), q_norm)
    k = rms_norm(k.reshape(bsz, s, H_C, DH_C), k_norm)
    v = v.reshape(bsz, s, H_C, DH_C)
    o = stick_breaking_attention(q, k, v).reshape(bsz, s, D_C)
    return (o * jax.nn.silu(z)) @ w_out


def setup_inputs(seed: int = 0) -> dict:
    key = jax.random.key(seed)
    ks = jax.random.split(key, 16)

    def nrm(k, shape, scale):
        return jax.random.normal(k, shape, jnp.float32) * scale

    return {
        "x": nrm(ks[0], (BATCH, SEQ, D_MODEL), 1.0),
        "c": nrm(ks[1], (BATCH, D_MODEL), 1.0),
        "ln_g": 1.0 + nrm(ks[2], (DEPTH, D_MODEL), 0.1),
        "ada_w": nrm(ks[3], (DEPTH, D_MODEL, 3 * D_MODEL), D_MODEL ** -0.5),
        "ada_b": nrm(ks[4], (DEPTH, 3 * D_MODEL), 0.1),
        "w_in_ab": nrm(ks[5], (N_EVEN, D_MODEL, IN_AB), D_MODEL ** -0.5),
        "conv_w": nrm(ks[6], (N_EVEN, CONV_W, D_A), CONV_W ** -0.5),
        "sg_norm": 1.0 + nrm(ks[7], (N_EVEN, G_B, DH_B), 0.1),
        "sg_w": nrm(ks[8], (N_EVEN, G_B, CHUNK, CHUNK), CHUNK ** -0.5),
        "sg_b": 1.0 + nrm(ks[9], (N_EVEN, G_B, CHUNK), 0.1),
        "w_out_ab": nrm(ks[10], (N_EVEN, D_A + D_B, D_MODEL), (D_A + D_B) ** -0.5),
        "w_in_c": nrm(ks[11], (N_ODD, D_MODEL, IN_C), D_MODEL ** -0.5),
        "q_norm": 1.0 + nrm(ks[12], (N_ODD, DH_C), 0.1),
        "k_norm": 1.0 + nrm(ks[13], (N_ODD, DH_C), 0.1),
        "w_out_c": nrm(ks[14], (N_ODD, D_C, D_MODEL), D_C ** -0.5),
    }


def reference(x, c, ln_g, ada_w, ada_b, w_in_ab, conv_w, sg_norm, sg_w, sg_b,
              w_out_ab, w_in_c, q_norm, k_norm, w_out_c):
    c_act = jax.nn.silu(c)
    for l in range(DEPTH):
        mod = c_act @ ada_w[l] + ada_b[l]
        shift, scale, gate = jnp.split(mod, 3, axis=-1)
        h = rms_norm(x, ln_g[l]) * (1.0 + scale[:, None, :]) + shift[:, None, :]
        i = l // 2
        if l % 2 == 0:
            out = conv_sgmlp_mixer(h, w_in_ab[i], conv_w[i], sg_norm[i], sg_w[i],
                                   sg_b[i], w_out_ab[i])
        else:
            out = stick_breaking_mixer(h, w_in_c[i], q_norm[i], k_norm[i], w_out_c[i])
        x = x + gate[:, None, :] * out
    return x
```

```python
import functools

import jax
import jax.numpy as jnp
from jax import lax
from jax.experimental import pallas as pl
from jax.experimental.pallas import tpu as pltpu

F32 = jnp.float32
BF16 = jnp.bfloat16

EPS = 1e-6
CONV_W = 3
DH_B = 128
CHUNK = 128
DH_C = 64
LOG2E = 1.4426950408889634

LANES = 128
SUBLANES = 8
MXU_TILE = 256
VMEM_LIMIT_BYTES = 56 * 1024 * 1024

COL_BLOCK = MXU_TILE
ROW_TILE = 512
ATTN_BLOCK = 256


def _const_spec(shape):
    nd = len(shape)
    return pl.BlockSpec(shape, lambda *_: (0,) * nd, pipeline_mode=pl.Buffered(1))


def _dot(a, b):
    return jnp.dot(a, b, preferred_element_type=F32)


def _silu(x):
    return x * jax.nn.sigmoid(x)


def _mod_body(c_ref, w_ref, b_ref, o_ref):
    c = c_ref[...]
    o_ref[0] = jnp.dot(_silu(c), w_ref[0], preferred_element_type=F32,
                       precision=lax.Precision.HIGHEST) + b_ref[0]


def _modulation(c, ada_w, ada_b):
    depth, d, n3 = ada_w.shape
    bsz = c.shape[0]
    rows = -(-bsz // SUBLANES) * SUBLANES
    c_pad = jnp.pad(c, ((0, rows - bsz), (0, 0)))
    tn = d
    mod = pl.pallas_call(
        _mod_body,
        grid=(depth, n3 // tn),
        in_specs=[
            pl.BlockSpec((rows, d), lambda l, n: (0, 0)),
            pl.BlockSpec((1, d, tn), lambda l, n: (l, 0, n)),
            pl.BlockSpec((1, 1, tn), lambda l, n: (l, 0, n)),
        ],
        out_specs=pl.BlockSpec((1, rows, tn), lambda l, n: (l, 0, n)),
        out_shape=jax.ShapeDtypeStruct((depth, rows, n3), F32),
        compiler_params=pltpu.CompilerParams(
            dimension_semantics=("arbitrary", "arbitrary"),
            vmem_limit_bytes=VMEM_LIMIT_BYTES),
        name="adaln_modulation",
    )(c_pad, ada_w, ada_b.reshape(depth, 1, n3))
    return mod[:, :bsz, :]


def _modulated_norm(x, g, scale, shift):
    ms = jnp.mean(x * x, axis=-1, keepdims=True)
    xn = x * lax.rsqrt(ms + EPS)
    return (xn * g) * (1.0 + scale) + shift


def _layer0_body(x_ref, shift_ref, scale_ref, gate_ref, g_ref, win_ref, convw_ref,
                 sgn_ref, sgw_ref, sgb_ref, wout_ref, o_ref,
                 h_s, u_s, carry_s, y_s):
    tm, d = x_ref.shape
    d_a = convw_ref.shape[1]
    d_b = sgb_ref.shape[1]
    cw = COL_BLOCK

    @pl.when(pl.program_id(1) == 0)
    def _():
        carry_s[...] = jnp.zeros_like(carry_s)

    x = x_ref[...]
    h_s[...] = _modulated_norm(x, g_ref[...], scale_ref[...], shift_ref[...]).astype(BF16)
    hb = h_s[...]

    for cb in range(d_a // cw):
        c0 = cb * cw
        bg = _dot(hb, win_ref[:, c0:c0 + cw])
        cg = _dot(hb, win_ref[:, d_a + c0:d_a + c0 + cw])
        xa = _dot(hb, win_ref[:, 2 * d_a + c0:2 * d_a + c0 + cw])
        za = _dot(hb, win_ref[:, 3 * d_a + c0:3 * d_a + c0 + cw])
        u = cg * xa
        u_s[0:SUBLANES, :] = carry_s[:, c0:c0 + cw]
        u_s[SUBLANES:SUBLANES + tm, :] = u
        carry_s[:, c0:c0 + cw] = u[tm - SUBLANES:tm, :]
        u1 = u_s[SUBLANES - 1:SUBLANES - 1 + tm, :]
        u2 = u_s[SUBLANES - 2:SUBLANES - 2 + tm, :]
        conv = (convw_ref[2:3, c0:c0 + cw] * u + convw_ref[1:2, c0:c0 + cw] * u1
                + convw_ref[0:1, c0:c0 + cw] * u2)
        y_s[:, c0:c0 + cw] = (bg * conv * _silu(za)).astype(BF16)

    row = lax.broadcasted_iota(jnp.int32, (CHUNK, CHUNK), 0)
    col = lax.broadcasted_iota(jnp.int32, (CHUNK, CHUNK), 1)
    tril = row >= col
    off = 4 * d_a
    for cb in range(d_b // cw):
        c0 = cb * cw
        ub = _dot(hb, win_ref[:, off + c0:off + c0 + cw])
        vb = _dot(hb, win_ref[:, off + d_b + c0:off + d_b + c0 + cw])
        zb = _dot(hb, win_ref[:, off + 2 * d_b + c0:off + 2 * d_b + c0 + cw])
        gz = ub * _silu(zb)
        for gg in range(cw // DH_B):
            grp = (c0 + gg * DH_B) // DH_B
            l0 = gg * DH_B
            vg = vb[:, l0:l0 + DH_B]
            ms = jnp.mean(vg * vg, axis=-1, keepdims=True)
            vn = (vg * lax.rsqrt(ms + EPS) * sgn_ref[grp:grp + 1, :]).astype(BF16)
            wt = jnp.where(tril, sgw_ref[grp], 0.0).astype(BF16)
            bias = sgb_ref[:, c0 + l0:c0 + l0 + DH_B]
            for ch in range(tm // CHUNK):
                r0 = ch * CHUNK
                sgate = _dot(wt, vn[r0:r0 + CHUNK, :]) + bias
                y_s[r0:r0 + CHUNK, d_a + c0 + l0:d_a + c0 + l0 + DH_B] = (
                    gz[r0:r0 + CHUNK, l0:l0 + DH_B] * sgate).astype(BF16)

    out = _dot(y_s[...], wout_ref[...])
    o_ref[...] = x + gate_ref[...] * out


def _layer0(x, shift, scale, gate, ln_g, w_in, conv_w, sg_norm, sg_w, sg_b, w_out):
    bsz, s, d = x.shape
    d_a = conv_w.shape[1]
    g_b = sg_norm.shape[0]
    d_b = g_b * DH_B
    tm = min(ROW_TILE, s)
    assert s % tm == 0 and tm % CHUNK == 0 and d_a % COL_BLOCK == 0 and d_b % COL_BLOCK == 0
    bias_full = jnp.broadcast_to(sg_b.T[:, :, None], (CHUNK, g_b, DH_B)).reshape(CHUNK, d_b)
    vec = lambda: pl.BlockSpec((None, 1, d), lambda b, i: (b, 0, 0))
    return pl.pallas_call(
        _layer0_body,
        grid=(bsz, s // tm),
        in_specs=[
            pl.BlockSpec((None, tm, d), lambda b, i: (b, i, 0)),
            vec(), vec(), vec(),
            _const_spec((1, d)),
            _const_spec(w_in.shape),
            _const_spec(conv_w.shape),
            _const_spec(sg_norm.shape),
            _const_spec(sg_w.shape),
            _const_spec(bias_full.shape),
            _const_spec(w_out.shape),
        ],
        out_specs=pl.BlockSpec((None, tm, d), lambda b, i: (b, i, 0)),
        out_shape=jax.ShapeDtypeStruct((bsz, s, d), F32),
        scratch_shapes=[
            pltpu.VMEM((tm, d), BF16),
            pltpu.VMEM((tm + SUBLANES, COL_BLOCK), F32),
            pltpu.VMEM((SUBLANES, d_a), F32),
            pltpu.VMEM((tm, d_a + d_b), BF16),
        ],
        compiler_params=pltpu.CompilerParams(
            dimension_semantics=("arbitrary", "arbitrary"),
            vmem_limit_bytes=VMEM_LIMIT_BYTES),
        name="layer0_conv_sgmlp",
    )(x, shift, scale, gate, ln_g.reshape(1, d), w_in.astype(BF16), conv_w, sg_norm, sg_w,
      bias_full, w_out.astype(BF16))


def _head_norm(t, bd, gain):
    tt = t * t
    hi = tt.astype(BF16)
    lo = (tt - hi.astype(F32)).astype(BF16)
    ss = _dot(hi, bd) + _dot(lo, bd)
    return t * lax.rsqrt(ss * (1.0 / DH_C) + EPS) * gain


def _inproj1_body(x_ref, shift_ref, scale_ref, g_ref, w_ref, qg_ref, kg_ref,
                  q_o, k_o, v_o, sz_o, h_s):
    tm, d = x_ref.shape
    cw = COL_BLOCK
    h_s[...] = _modulated_norm(x_ref[...], g_ref[...], scale_ref[...], shift_ref[...]).astype(BF16)
    hb = h_s[...]
    row = lax.broadcasted_iota(jnp.int32, (cw, cw), 0)
    col = lax.broadcasted_iota(jnp.int32, (cw, cw), 1)
    bd = jnp.where((row // DH_C) == (col // DH_C), 1.0, 0.0).astype(BF16)
    q_scale = -LOG2E * (DH_C ** -0.5)
    for cb in range(d // cw):
        c0 = cb * cw
        q = _dot(hb, w_ref[:, c0:c0 + cw])
        q_o[:, c0:c0 + cw] = (_head_norm(q, bd, qg_ref[:, c0:c0 + cw]) * q_scale).astype(BF16)
        k = _dot(hb, w_ref[:, d + c0:d + c0 + cw])
        k_o[:, c0:c0 + cw] = _head_norm(k, bd, kg_ref[:, c0:c0 + cw]).astype(BF16)
        v_o[:, c0:c0 + cw] = _dot(hb, w_ref[:, 2 * d + c0:2 * d + c0 + cw]).astype(BF16)
        z = _dot(hb, w_ref[:, 3 * d + c0:3 * d + c0 + cw])
        sz_o[:, c0:c0 + cw] = _silu(z).astype(BF16)


def _inproj1(x, shift, scale, ln_g, w_in, q_norm, k_norm):
    bsz, s, d = x.shape
    tm = min(ROW_TILE, s)
    assert s % tm == 0 and d % COL_BLOCK == 0 and COL_BLOCK % DH_C == 0
    heads = d // DH_C
    vec = lambda: pl.BlockSpec((None, 1, d), lambda b, i: (b, 0, 0))
    tile = lambda: pl.BlockSpec((None, tm, d), lambda b, i: (b, i, 0))
    out = jax.ShapeDtypeStruct((bsz, s, d), BF16)
    return pl.pallas_call(
        _inproj1_body,
        grid=(bsz, s // tm),
        in_specs=[tile(), vec(), vec(), _const_spec((1, d)), _const_spec(w_in.shape),
                  _const_spec((1, d)), _const_spec((1, d))],
        out_specs=[tile(), tile(), tile(), tile()],
        out_shape=[out, out, out, out],
        scratch_shapes=[pltpu.VMEM((tm, d), BF16)],
        compiler_params=pltpu.CompilerParams(
            dimension_semantics=("arbitrary", "arbitrary"),
            vmem_limit_bytes=VMEM_LIMIT_BYTES),
        name="layer1_qkvz_proj",
    )(x, shift, scale, ln_g.reshape(1, d), w_in.astype(BF16),
      jnp.tile(q_norm, heads).reshape(1, d), jnp.tile(k_norm, heads).reshape(1, d))


def _attn_body(q_ref, k_ref, v_ref, sz_ref, o_ref):
    s, width = q_ref.shape
    blk = ATTN_BLOCK
    nblk = s // blk
    heads = width // DH_C

    row = lax.broadcasted_iota(jnp.int32, (blk, blk), 0)
    col = lax.broadcasted_iota(jnp.int32, (blk, blk), 1)
    later = jnp.where(row > col, 1.0, 0.0).astype(BF16)
    causal = col < row
    lane = lax.broadcasted_iota(jnp.int32, (blk, width), 1)
    head_lanes = [(lane >= hh * DH_C) & (lane < (hh + 1) * DH_C) for hh in range(heads)]

    def tile(qm, kb, vb, carry, diag):
        zn = lax.dot_general(qm, kb, (((1,), (1,)), ((), ())), preferred_element_type=F32)
        lm = jnp.minimum(zn, 0.0) - jnp.log2(1.0 + jnp.exp2(-jnp.abs(zn)))
        lb = lm - zn
        if diag:
            lm = jnp.where(causal, lm, 0.0)
        suf = _dot(lm.astype(BF16), later)
        arg = lb + suf + jnp.concatenate([carry] * (blk // width), axis=1)
        w = jnp.exp2(arg)
        if diag:
            w = jnp.where(causal, w, 0.0)
        pv = _dot(w.astype(BF16), vb)
        total = suf[:, 0:1] + lm[:, 0:1]
        return pv, carry + jnp.broadcast_to(total, carry.shape)

    def q_block(i, _):
        q0 = pl.multiple_of(i * blk, blk)
        q = q_ref[pl.ds(q0, blk), :]
        qms = [jnp.where(head_lanes[hh], q, jnp.zeros_like(q)) for hh in range(heads)]

        def step(k0, state, diag):
            kb = k_ref[pl.ds(k0, blk), :]
            vb = v_ref[pl.ds(k0, blk), :]
            acc, carries = state
            new_carries = []
            for hh in range(heads):
                pv, c_new = tile(qms[hh], kb, vb, carries[hh], diag)
                acc = acc + jnp.where(head_lanes[hh], pv, 0.0)
                new_carries.append(c_new)
            return acc, tuple(new_carries)

        zeros = jnp.zeros((blk, width), F32)
        state = step(q0, (zeros, (zeros,) * heads), True)

        def key_block(n, state):
            k0 = pl.multiple_of((i - 1 - n) * blk, blk)
            return step(k0, state, False)

        acc, _ = lax.fori_loop(0, i, key_block, state)
        o_ref[pl.ds(q0, blk), :] = (acc * sz_ref[pl.ds(q0, blk), :].astype(F32)).astype(BF16)
        return 0

    lax.fori_loop(0, nblk, q_block, 0)


def _attention(q, k, v, sz):
    bsz, s, d = q.shape
    assert s % ATTN_BLOCK == 0 and d % LANES == 0 and LANES % DH_C == 0
    spec = lambda: pl.BlockSpec((None, s, LANES), lambda b, h: (b, 0, h))
    return pl.pallas_call(
        _attn_body,
        grid=(bsz, d // LANES),
        in_specs=[spec(), spec(), spec(), spec()],
        out_specs=spec(),
        out_shape=jax.ShapeDtypeStruct((bsz, s, d), BF16),
        compiler_params=pltpu.CompilerParams(
            dimension_semantics=("arbitrary", "arbitrary"),
            vmem_limit_bytes=VMEM_LIMIT_BYTES),
        name="stick_breaking_attention",
    )(q, k, v, sz)


def _outproj_body(y_ref, x_ref, gate_ref, w_ref, o_ref):
    o_ref[...] = x_ref[...] + gate_ref[...] * _dot(y_ref[...], w_ref[...])


def _outproj(y, x, gate, w_out):
    bsz, s, d = x.shape
    tm = min(ROW_TILE, s)
    assert s % tm == 0
    return pl.pallas_call(
        _outproj_body,
        grid=(bsz, s // tm),
        in_specs=[
            pl.BlockSpec((None, tm, y.shape[2]), lambda b, i: (b, i, 0)),
            pl.BlockSpec((None, tm, d), lambda b, i: (b, i, 0)),
            pl.BlockSpec((None, 1, d), lambda b, i: (b, 0, 0)),
            _const_spec(w_out.shape),
        ],
        out_specs=pl.BlockSpec((None, tm, d), lambda b, i: (b, i, 0)),
        out_shape=jax.ShapeDtypeStruct((bsz, s, d), F32),
        compiler_params=pltpu.CompilerParams(
            dimension_semantics=("arbitrary", "arbitrary"),
            vmem_limit_bytes=VMEM_LIMIT_BYTES),
        name="layer1_out_proj",
    )(y, x, gate, w_out.astype(BF16))


def kernel(x, c, ln_g, ada_w, ada_b, w_in_ab, conv_w, sg_norm, sg_w, sg_b, w_out_ab,
           w_in_c, q_norm, k_norm, w_out_c):
    bsz, _, d = x.shape
    depth = ada_w.shape[0]
    mod = _modulation(c, ada_w, ada_b)
    for l in range(depth):
        shift, scale, gate = (mod[l, :, j * d:(j + 1) * d].reshape(bsz, 1, d) for j in range(3))
        i = l // 2
        if l % 2 == 0:
            x = _layer0(x, shift, scale, gate, ln_g[l], w_in_ab[i], conv_w[i], sg_norm[i],
                        sg_w[i], sg_b[i], w_out_ab[i])
        else:
            q, k, v, sz = _inproj1(x, shift, scale, ln_g[l], w_in_c[i], q_norm[i], k_norm[i])
            y = _attention(q, k, v, sz)
            x = _outproj(y, x, gate, w_out_c[i])
    return x
```

```python
import functools

import jax
import jax.numpy as jnp
from jax import lax
from jax.experimental import pallas as pl
from jax.experimental.pallas import tpu as pltpu

F32 = jnp.float32
BF16 = jnp.bfloat16

EPS = 1e-6
CONV_W = 3
DH_B = 128
CHUNK = 128
DH_C = 64
LOG2E = 1.4426950408889634

LANES = 128
SUBLANES = 8
MXU_TILE = 256
VMEM_LIMIT_BYTES = 56 * 1024 * 1024

COL_BLOCK = MXU_TILE
ROW_TILE = 512
ATTN_BLOCK = 256


def _const_spec(shape):
    nd = len(shape)
    return pl.BlockSpec(shape, lambda *_: (0,) * nd, pipeline_mode=pl.Buffered(1))


def _dot(a, b):
    return jnp.dot(a, b, preferred_element_type=F32)


def _silu(x):
    return x * jax.nn.sigmoid(x)


def _mod_body(c_ref, w_ref, b_ref, o_ref):
    c = c_ref[...]
    o_ref[0] = jnp.dot(_silu(c), w_ref[0], preferred_element_type=F32,
                       precision=lax.Precision.HIGHEST) + b_ref[0]


def _modulation(c, ada_w, ada_b):
    depth, d, n3 = ada_w.shape
    bsz = c.shape[0]
    rows = -(-bsz // SUBLANES) * SUBLANES
    c_pad = jnp.pad(c, ((0, rows - bsz), (0, 0)))
    tn = d
    mod = pl.pallas_call(
        _mod_body,
        grid=(depth, n3 // tn),
        in_specs=[
            pl.BlockSpec((rows, d), lambda l, n: (0, 0)),
            pl.BlockSpec((1, d, tn), lambda l, n: (l, 0, n)),
            pl.BlockSpec((1, 1, tn), lambda l, n: (l, 0, n)),
        ],
        out_specs=pl.BlockSpec((1, rows, tn), lambda l, n: (l, 0, n)),
        out_shape=jax.ShapeDtypeStruct((depth, rows, n3), F32),
        compiler_params=pltpu.CompilerParams(
            dimension_semantics=("arbitrary", "arbitrary"),
            vmem_limit_bytes=VMEM_LIMIT_BYTES),
        name="adaln_modulation",
    )(c_pad, ada_w, ada_b.reshape(depth, 1, n3))
    return mod[:, :bsz, :]


def _modulated_norm(x, g, scale, shift):
    ms = jnp.mean(x * x, axis=-1, keepdims=True)
    xn = x * lax.rsqrt(ms + EPS)
    return (xn * g) * (1.0 + scale) + shift


def _layer0_body(x_ref, shift_ref, scale_ref, gate_ref, g_ref, win_ref, convw_ref,
                 sgn_ref, sgw_ref, sgb_ref, wout_ref, o_ref,
                 h_s, u_s, carry_s, y_s):
    tm, d = x_ref.shape
    d_a = convw_ref.shape[1]
    d_b = sgb_ref.shape[1]
    cw = COL_BLOCK

    @pl.when(pl.program_id(1) == 0)
    def _():
        carry_s[...] = jnp.zeros_like(carry_s)

    x = x_ref[...]
    h_s[...] = _modulated_norm(x, g_ref[...], scale_ref[...], shift_ref[...]).astype(BF16)
    hb = h_s[...]

    for cb in range(d_a // cw):
        c0 = cb * cw
        bg = _dot(hb, win_ref[:, c0:c0 + cw])
        cg = _dot(hb, win_ref[:, d_a + c0:d_a + c0 + cw])
        xa = _dot(hb, win_ref[:, 2 * d_a + c0:2 * d_a + c0 + cw])
        za = _dot(hb, win_ref[:, 3 * d_a + c0:3 * d_a + c0 + cw])
        u = cg * xa
        u_s[0:SUBLANES, :] = carry_s[:, c0:c0 + cw]
        u_s[SUBLANES:SUBLANES + tm, :] = u
        carry_s[:, c0:c0 + cw] = u[tm - SUBLANES:tm, :]
        u1 = u_s[SUBLANES - 1:SUBLANES - 1 + tm, :]
        u2 = u_s[SUBLANES - 2:SUBLANES - 2 + tm, :]
        conv = (convw_ref[2:3, c0:c0 + cw] * u + convw_ref[1:2, c0:c0 + cw] * u1
                + convw_ref[0:1, c0:c0 + cw] * u2)
        y_s[:, c0:c0 + cw] = (bg * conv * _silu(za)).astype(BF16)

    row = lax.broadcasted_iota(jnp.int32, (CHUNK, CHUNK), 0)
    col = lax.broadcasted_iota(jnp.int32, (CHUNK, CHUNK), 1)
    tril = row >= col
    off = 4 * d_a
    for cb in range(d_b // cw):
        c0 = cb * cw
        ub = _dot(hb, win_ref[:, off + c0:off + c0 + cw])
        vb = _dot(hb, win_ref[:, off + d_b + c0:off + d_b + c0 + cw])
        zb = _dot(hb, win_ref[:, off + 2 * d_b + c0:off + 2 * d_b + c0 + cw])
        gz = ub * _silu(zb)
        for gg in range(cw // DH_B):
            grp = (c0 + gg * DH_B) // DH_B
            l0 = gg * DH_B
            vg = vb[:, l0:l0 + DH_B]
            ms = jnp.mean(vg * vg, axis=-1, keepdims=True)
            vn = (vg * lax.rsqrt(ms + EPS) * sgn_ref[grp:grp + 1, :]).astype(BF16)
            wt = jnp.where(tril, sgw_ref[grp], 0.0).astype(BF16)
            bias = sgb_ref[:, c0 + l0:c0 + l0 + DH_B]
            for ch in range(tm // CHUNK):
                r0 = ch * CHUNK
                sgate = _dot(wt, vn[r0:r0 + CHUNK, :]) + bias
                y_s[r0:r0 + CHUNK, d_a + c0 + l0:d_a + c0 + l0 + DH_B] = (
                    gz[r0:r0 + CHUNK, l0:l0 + DH_B] * sgate).astype(BF16)

    out = _dot(y_s[...], wout_ref[...])
    o_ref[...] = x + gate_ref[...] * out


def _layer0(x, shift, scale, gate, ln_g, w_in, conv_w, sg_norm, sg_w, sg_b, w_out):
    bsz, s, d = x.shape
    d_a = conv_w.shape[1]
    g_b = sg_norm.shape[0]
    d_b = g_b * DH_B
    tm = min(ROW_TILE, s)
    assert s % tm == 0 and tm % CHUNK == 0 and d_a % COL_BLOCK == 0 and d_b % COL_BLOCK == 0
    bias_full = jnp.broadcast_to(sg_b.T[:, :, None], (CHUNK, g_b, DH_B)).reshape(CHUNK, d_b)
    vec = lambda: pl.BlockSpec((None, 1, d), lambda b, i: (b, 0, 0))
    return pl.pallas_call(
        _layer0_body,
        grid=(bsz, s // tm),
        in_specs=[
            pl.BlockSpec((None, tm, d), lambda b, i: (b, i, 0)),
            vec(), vec(), vec(),
            _const_spec((1, d)),
            _const_spec(w_in.shape),
            _const_spec(conv_w.shape),
            _const_spec(sg_norm.shape),
            _const_spec(sg_w.shape),
            _const_spec(bias_full.shape),
            _const_spec(w_out.shape),
        ],
        out_specs=pl.BlockSpec((None, tm, d), lambda b, i: (b, i, 0)),
        out_shape=jax.ShapeDtypeStruct((bsz, s, d), F32),
        scratch_shapes=[
            pltpu.VMEM((tm, d), BF16),
            pltpu.VMEM((tm + SUBLANES, COL_BLOCK), F32),
            pltpu.VMEM((SUBLANES, d_a), F32),
            pltpu.VMEM((tm, d_a + d_b), BF16),
        ],
        compiler_params=pltpu.CompilerParams(
            dimension_semantics=("arbitrary", "arbitrary"),
            vmem_limit_bytes=VMEM_LIMIT_BYTES),
        name="layer0_conv_sgmlp",
    )(x, shift, scale, gate, ln_g.reshape(1, d), w_in.astype(BF16), conv_w, sg_norm, sg_w,
      bias_full, w_out.astype(BF16))


def _head_norm(t, bd, gain):
    tt = t * t
    hi = tt.astype(BF16)
    lo = (tt - hi.astype(F32)).astype(BF16)
    ss = _dot(hi, bd) + _dot(lo, bd)
    return t * lax.rsqrt(ss * (1.0 / DH_C) + EPS) * gain


def _inproj1_body(x_ref, shift_ref, scale_ref, g_ref, w_ref, qg_ref, kg_ref,
                  q_o, k_o, v_o, sz_o, h_s):
    tm, d = x_ref.shape
    cw = COL_BLOCK
    h_s[...] = _modulated_norm(x_ref[...], g_ref[...], scale_ref[...], shift_ref[...]).astype(BF16)
    hb = h_s[...]
    row = lax.broadcasted_iota(jnp.int32, (cw, cw), 0)
    col = lax.broadcasted_iota(jnp.int32, (cw, cw), 1)
    bd = jnp.where((row // DH_C) == (col // DH_C), 1.0, 0.0).astype(BF16)
    q_scale = -LOG2E * (DH_C ** -0.5)
    for cb in range(d // cw):
        c0 = cb * cw
        q = _dot(hb, w_ref[:, c0:c0 + cw])
        q_o[:, c0:c0 + cw] = (_head_norm(q, bd, qg_ref[:, c0:c0 + cw]) * q_scale).astype(BF16)
        k = _dot(hb, w_ref[:, d + c0:d + c0 + cw])
        k_o[:, c0:c0 + cw] = _head_norm(k, bd, kg_ref[:, c0:c0 + cw]).astype(BF16)
        v_o[:, c0:c0 + cw] = _dot(hb, w_ref[:, 2 * d + c0:2 * d + c0 + cw]).astype(BF16)
        z = _dot(hb, w_ref[:, 3 * d + c0:3 * d + c0 + cw])
        sz_o[:, c0:c0 + cw] = _silu(z).astype(BF16)


def _inproj1(x, shift, scale, ln_g, w_in, q_norm, k_norm):
    bsz, s, d = x.shape
    tm = min(ROW_TILE, s)
    assert s % tm == 0 and d % COL_BLOCK == 0 and COL_BLOCK % DH_C == 0
    heads = d // DH_C
    vec = lambda: pl.BlockSpec((None, 1, d), lambda b, i: (b, 0, 0))
    tile = lambda: pl.BlockSpec((None, tm, d), lambda b, i: (b, i, 0))
    out = jax.ShapeDtypeStruct((bsz, s, d), BF16)
    return pl.pallas_call(
        _inproj1_body,
        grid=(bsz, s // tm),
        in_specs=[tile(), vec(), vec(), _const_spec((1, d)), _const_spec(w_in.shape),
                  _const_spec((1, d)), _const_spec((1, d))],
        out_specs=[tile(), tile(), tile(), tile()],
        out_shape=[out, out, out, out],
        scratch_shapes=[pltpu.VMEM((tm, d), BF16)],
        compiler_params=pltpu.CompilerParams(
            dimension_semantics=("arbitrary", "arbitrary"),
            vmem_limit_bytes=VMEM_LIMIT_BYTES),
        name="layer1_qkvz_proj",
    )(x, shift, scale, ln_g.reshape(1, d), w_in.astype(BF16),
      jnp.tile(q_norm, heads).reshape(1, d), jnp.tile(k_norm, heads).reshape(1, d))


NEG_BIG = -1e30


def _attn_body(q_ref, k_ref, v_ref, sz_ref, o_ref, qm_s, c_s, acc_s, zn_s, p_s, tot_s):
    s, width = q_ref.shape
    blk = ATTN_BLOCK
    nblk = s // blk
    heads = width // DH_C
    rows = heads * blk

    row = lax.broadcasted_iota(jnp.int32, (blk, blk), 0)
    col = lax.broadcasted_iota(jnp.int32, (blk, blk), 1)
    later = jnp.where(row > col, 1.0, 0.0).astype(BF16)
    causal = jnp.concatenate([col < row] * heads, axis=0)
    lane = lax.broadcasted_iota(jnp.int32, (blk, width), 1)

    def prep(i, _):
        q = q_ref[pl.ds(pl.multiple_of(i * blk, blk), blk), :]
        for hh in range(heads):
            keep = (lane >= hh * DH_C) & (lane < (hh + 1) * DH_C)
            qm_s[i, hh * blk:(hh + 1) * blk, :] = jnp.where(keep, q, jnp.zeros_like(q))
        c_s[i] = jnp.zeros((rows, width), F32)
        acc_s[i] = jnp.zeros((blk, width), F32)
        return 0

    lax.fori_loop(0, nblk, prep, 0)

    def scores(idx, slot):
        kb = k_ref[pl.ds(pl.multiple_of(idx[1] * blk, blk), blk), :]
        zn_s[slot] = lax.dot_general(qm_s[idx[0]], kb, (((1,), (1,)), ((), ())),
                                     preferred_element_type=F32)

    def gate_logs(slot, diag):
        zn = zn_s[slot]
        lm = jnp.minimum(zn, 0.0) - jnp.log2(1.0 + jnp.exp2(-jnp.abs(zn)))
        lb = lm - zn
        if diag:
            lm = jnp.where(causal, lm, 0.0)
        suf = _dot(lm.astype(BF16), later)
        p = lb + suf
        if diag:
            p = jnp.where(causal, p, NEG_BIG)
        p_s[slot] = p
        tot_s[slot] = jnp.broadcast_to(suf[:, 0:1] + lm[:, 0:1], (rows, width))

    def weigh(idx, slot):
        vb = v_ref[pl.ds(pl.multiple_of(idx[1] * blk, blk), blk), :]
        c = c_s[idx[0]]
        w = jnp.exp2(p_s[slot] + jnp.concatenate([c] * (blk // width), axis=1))
        pv = _dot(w.astype(BF16), vb)
        c_s[idx[0]] = c + tot_s[slot]
        out = pv[0:blk, :]
        for hh in range(1, heads):
            out = jnp.where(lane >= hh * DH_C, pv[hh * blk:(hh + 1) * blk, :], out)
        acc_s[idx[0]] += out

    def run_tiles(ntiles, first, step, diag):
        if ntiles == 0:
            return
        scores(first, 0)
        if ntiles == 1:
            gate_logs(0, diag)
            weigh(first, 0)
            return
        second = step(first)
        gate_logs(0, diag)
        scores(second, 1)

        def one_step(ia, ib, par):
            ic = step(ib)
            weigh(ia, par)
            gate_logs(1 - par, diag)
            scores(ic, par)
            return ib, ic

        ia, ib = first, second
        par = 0
        if ntiles % 2 == 1:
            ia, ib = one_step(ia, ib, par)
            par = 1 - par

        def pair(_, carry):
            return one_step(*one_step(*carry, par), 1 - par)

        ia, ib = lax.fori_loop(0, (ntiles - 2) // 2, pair, (ia, ib))
        last = ntiles % 2
        weigh(ia, last)
        gate_logs(1 - last, diag)
        weigh(ib, 1 - last)

    one = jnp.int32(1)
    zero = jnp.int32(0)
    run_tiles(nblk, (zero, zero), lambda ij: (ij[0] + 1, ij[1] + 1), True)
    run_tiles(nblk * (nblk - 1) // 2, (one, zero),
              lambda ij: (jnp.where(ij[1] > 0, ij[0], ij[0] + 1),
                          jnp.where(ij[1] > 0, ij[1] - 1, ij[0])), False)

    def finish(i, _):
        q0 = pl.multiple_of(i * blk, blk)
        o_ref[pl.ds(q0, blk), :] = (acc_s[i] * sz_ref[pl.ds(q0, blk), :].astype(F32)).astype(BF16)
        return 0

    lax.fori_loop(0, nblk, finish, 0)


def _attention(q, k, v, sz):
    bsz, s, d = q.shape
    assert s % ATTN_BLOCK == 0 and d % LANES == 0 and LANES % DH_C == 0
    nblk = s // ATTN_BLOCK
    heads = LANES // DH_C
    spec = lambda: pl.BlockSpec((None, s, LANES), lambda b, h: (b, 0, h))
    return pl.pallas_call(
        _attn_body,
        grid=(bsz, d // LANES),
        in_specs=[spec(), spec(), spec(), spec()],
        out_specs=spec(),
        out_shape=jax.ShapeDtypeStruct((bsz, s, d), BF16),
        scratch_shapes=[
            pltpu.VMEM((nblk, heads * ATTN_BLOCK, LANES), BF16),
            pltpu.VMEM((nblk, heads * ATTN_BLOCK, LANES), F32),
            pltpu.VMEM((nblk, ATTN_BLOCK, LANES), F32),
            pltpu.VMEM((2, heads * ATTN_BLOCK, ATTN_BLOCK), F32),
            pltpu.VMEM((2, heads * ATTN_BLOCK, ATTN_BLOCK), F32),
            pltpu.VMEM((2, heads * ATTN_BLOCK, LANES), F32),
        ],
        compiler_params=pltpu.CompilerParams(
            dimension_semantics=("arbitrary", "arbitrary"),
            vmem_limit_bytes=VMEM_LIMIT_BYTES),
        name="stick_breaking_attention",
    )(q, k, v, sz)


def _outproj_body(y_ref, x_ref, gate_ref, w_ref, o_ref):
    o_ref[...] = x_ref[...] + gate_ref[...] * _dot(y_ref[...], w_ref[...])


def _outproj(y, x, gate, w_out):
    bsz, s, d = x.shape
    tm = min(ROW_TILE, s)
    assert s % tm == 0
    return pl.pallas_call(
        _outproj_body,
        grid=(bsz, s // tm),
        in_specs=[
            pl.BlockSpec((None, tm, y.shape[2]), lambda b, i: (b, i, 0)),
            pl.BlockSpec((None, tm, d), lambda b, i: (b, i, 0)),
            pl.BlockSpec((None, 1, d), lambda b, i: (b, 0, 0)),
            _const_spec(w_out.shape),
        ],
        out_specs=pl.BlockSpec((None, tm, d), lambda b, i: (b, i, 0)),
        out_shape=jax.ShapeDtypeStruct((bsz, s, d), F32),
        compiler_params=pltpu.CompilerParams(
            dimension_semantics=("arbitrary", "arbitrary"),
            vmem_limit_bytes=VMEM_LIMIT_BYTES),
        name="layer1_out_proj",
    )(y, x, gate, w_out.astype(BF16))


def kernel(x, c, ln_g, ada_w, ada_b, w_in_ab, conv_w, sg_norm, sg_w, sg_b, w_out_ab,
           w_in_c, q_norm, k_norm, w_out_c):
    bsz, _, d = x.shape
    depth = ada_w.shape[0]
    mod = _modulation(c, ada_w, ada_b)
    for l in range(depth):
        shift, scale, gate = (mod[l, :, j * d:(j + 1) * d].reshape(bsz, 1, d) for j in range(3))
        i = l // 2
        if l % 2 == 0:
            x = _layer0(x, shift, scale, gate, ln_g[l], w_in_ab[i], conv_w[i], sg_norm[i],
                        sg_w[i], sg_b[i], w_out_ab[i])
        else:
            q, k, v, sz = _inproj1(x, shift, scale, ln_g[l], w_in_c[i], q_norm[i], k_norm[i])
            y = _attention(q, k, v, sz)
            x = _outproj(y, x, gate, w_out_c[i])
    return x
```

```python
import functools

import jax
import jax.numpy as jnp
from jax import lax
from jax.experimental import pallas as pl
from jax.experimental.pallas import tpu as pltpu

F32 = jnp.float32
BF16 = jnp.bfloat16

EPS = 1e-6
CONV_W = 3
DH_B = 128
CHUNK = 128
DH_C = 64
LOG2E = 1.4426950408889634

LANES = 128
SUBLANES = 8
MXU_TILE = 256
VMEM_LIMIT_BYTES = 56 * 1024 * 1024

COL_BLOCK = MXU_TILE
ROW_TILE = 512
ATTN_BLOCK = 256


def _const_spec(shape):
    nd = len(shape)
    return pl.BlockSpec(shape, lambda *_: (0,) * nd, pipeline_mode=pl.Buffered(1))


def _dot(a, b):
    return jnp.dot(a, b, preferred_element_type=F32)


def _silu(x):
    return x * jax.nn.sigmoid(x)


def _mod_body(c_ref, w_ref, b_ref, o_ref):
    c = c_ref[...]
    o_ref[0] = jnp.dot(_silu(c), w_ref[0], preferred_element_type=F32,
                       precision=lax.Precision.HIGHEST) + b_ref[0]


def _modulation(c, ada_w, ada_b):
    depth, d, n3 = ada_w.shape
    bsz = c.shape[0]
    rows = -(-bsz // SUBLANES) * SUBLANES
    c_pad = jnp.pad(c, ((0, rows - bsz), (0, 0)))
    tn = d
    mod = pl.pallas_call(
        _mod_body,
        grid=(depth, n3 // tn),
        in_specs=[
            pl.BlockSpec((rows, d), lambda l, n: (0, 0)),
            pl.BlockSpec((1, d, tn), lambda l, n: (l, 0, n)),
            pl.BlockSpec((1, 1, tn), lambda l, n: (l, 0, n)),
        ],
        out_specs=pl.BlockSpec((1, rows, tn), lambda l, n: (l, 0, n)),
        out_shape=jax.ShapeDtypeStruct((depth, rows, n3), F32),
        compiler_params=pltpu.CompilerParams(
            dimension_semantics=("arbitrary", "arbitrary"),
            vmem_limit_bytes=VMEM_LIMIT_BYTES),
        name="adaln_modulation",
    )(c_pad, ada_w, ada_b.reshape(depth, 1, n3))
    return mod[:, :bsz, :]


def _modulated_norm(x, g, scale, shift):
    ms = jnp.mean(x * x, axis=-1, keepdims=True)
    xn = x * lax.rsqrt(ms + EPS)
    return (xn * g) * (1.0 + scale) + shift


def _layer0_body(x_ref, shift_ref, scale_ref, gate_ref, g_ref, win_ref, convw_ref,
                 sgn_ref, sgw_ref, sgb_ref, wout_ref, o_ref,
                 h_s, u_s, carry_s, y_s):
    tm, d = x_ref.shape
    d_a = convw_ref.shape[1]
    d_b = sgb_ref.shape[1]
    cw = COL_BLOCK

    @pl.when(pl.program_id(1) == 0)
    def _():
        carry_s[...] = jnp.zeros_like(carry_s)

    x = x_ref[...]
    h_s[...] = _modulated_norm(x, g_ref[...], scale_ref[...], shift_ref[...]).astype(BF16)
    hb = h_s[...]

    for cb in range(d_a // cw):
        c0 = cb * cw
        bg = _dot(hb, win_ref[:, c0:c0 + cw])
        cg = _dot(hb, win_ref[:, d_a + c0:d_a + c0 + cw])
        xa = _dot(hb, win_ref[:, 2 * d_a + c0:2 * d_a + c0 + cw])
        za = _dot(hb, win_ref[:, 3 * d_a + c0:3 * d_a + c0 + cw])
        u = cg * xa
        u_s[0:SUBLANES, :] = carry_s[:, c0:c0 + cw]
        u_s[SUBLANES:SUBLANES + tm, :] = u
        carry_s[:, c0:c0 + cw] = u[tm - SUBLANES:tm, :]
        u1 = u_s[SUBLANES - 1:SUBLANES - 1 + tm, :]
        u2 = u_s[SUBLANES - 2:SUBLANES - 2 + tm, :]
        conv = (convw_ref[2:3, c0:c0 + cw] * u + convw_ref[1:2, c0:c0 + cw] * u1
                + convw_ref[0:1, c0:c0 + cw] * u2)
        y_s[:, c0:c0 + cw] = (bg * conv * _silu(za)).astype(BF16)

    row = lax.broadcasted_iota(jnp.int32, (CHUNK, CHUNK), 0)
    col = lax.broadcasted_iota(jnp.int32, (CHUNK, CHUNK), 1)
    tril = row >= col
    off = 4 * d_a
    for cb in range(d_b // cw):
        c0 = cb * cw
        ub = _dot(hb, win_ref[:, off + c0:off + c0 + cw])
        vb = _dot(hb, win_ref[:, off + d_b + c0:off + d_b + c0 + cw])
        zb = _dot(hb, win_ref[:, off + 2 * d_b + c0:off + 2 * d_b + c0 + cw])
        gz = ub * _silu(zb)
        for gg in range(cw // DH_B):
            grp = (c0 + gg * DH_B) // DH_B
            l0 = gg * DH_B
            vg = vb[:, l0:l0 + DH_B]
            ms = jnp.mean(vg * vg, axis=-1, keepdims=True)
            vn = (vg * lax.rsqrt(ms + EPS) * sgn_ref[grp:grp + 1, :]).astype(BF16)
            wt = jnp.where(tril, sgw_ref[grp], 0.0).astype(BF16)
            bias = sgb_ref[:, c0 + l0:c0 + l0 + DH_B]
            for ch in range(tm // CHUNK):
                r0 = ch * CHUNK
                sgate = _dot(wt, vn[r0:r0 + CHUNK, :]) + bias
                y_s[r0:r0 + CHUNK, d_a + c0 + l0:d_a + c0 + l0 + DH_B] = (
                    gz[r0:r0 + CHUNK, l0:l0 + DH_B] * sgate).astype(BF16)

    out = _dot(y_s[...], wout_ref[...])
    o_ref[...] = x + gate_ref[...] * out


def _layer0(x, shift, scale, gate, ln_g, w_in, conv_w, sg_norm, sg_w, sg_b, w_out):
    bsz, s, d = x.shape
    d_a = conv_w.shape[1]
    g_b = sg_norm.shape[0]
    d_b = g_b * DH_B
    tm = min(ROW_TILE, s)
    assert s % tm == 0 and tm % CHUNK == 0 and d_a % COL_BLOCK == 0 and d_b % COL_BLOCK == 0
    bias_full = jnp.broadcast_to(sg_b.T[:, :, None], (CHUNK, g_b, DH_B)).reshape(CHUNK, d_b)
    vec = lambda: pl.BlockSpec((None, 1, d), lambda b, i: (b, 0, 0))
    return pl.pallas_call(
        _layer0_body,
        grid=(bsz, s // tm),
        in_specs=[
            pl.BlockSpec((None, tm, d), lambda b, i: (b, i, 0)),
            vec(), vec(), vec(),
            _const_spec((1, d)),
            _const_spec(w_in.shape),
            _const_spec(conv_w.shape),
            _const_spec(sg_norm.shape),
            _const_spec(sg_w.shape),
            _const_spec(bias_full.shape),
            _const_spec(w_out.shape),
        ],
        out_specs=pl.BlockSpec((None, tm, d), lambda b, i: (b, i, 0)),
        out_shape=jax.ShapeDtypeStruct((bsz, s, d), F32),
        scratch_shapes=[
            pltpu.VMEM((tm, d), BF16),
            pltpu.VMEM((tm + SUBLANES, COL_BLOCK), F32),
            pltpu.VMEM((SUBLANES, d_a), F32),
            pltpu.VMEM((tm, d_a + d_b), BF16),
        ],
        compiler_params=pltpu.CompilerParams(
            dimension_semantics=("arbitrary", "arbitrary"),
            vmem_limit_bytes=VMEM_LIMIT_BYTES),
        name="layer0_conv_sgmlp",
    )(x, shift, scale, gate, ln_g.reshape(1, d), w_in.astype(BF16), conv_w, sg_norm, sg_w,
      bias_full, w_out.astype(BF16))


def _inproj1_body(x_ref, shift_ref, scale_ref, g_ref, w_ref, qg_ref, kg_ref,
                  q_o, k_o, v_o, sz_o, h_s, raw_s):
    tm, d = x_ref.shape
    cw = COL_BLOCK
    h_s[...] = _modulated_norm(x_ref[...], g_ref[...], scale_ref[...], shift_ref[...]).astype(BF16)
    hb = h_s[...]
    for cb in range(2 * d // cw):
        raw_s[:, cb * cw:(cb + 1) * cw] = _dot(hb, w_ref[:, cb * cw:(cb + 1) * cw])
    for cb in range(d // cw):
        c0 = cb * cw
        v_o[:, c0:c0 + cw] = _dot(hb, w_ref[:, 2 * d + c0:2 * d + c0 + cw]).astype(BF16)
        sz_o[:, c0:c0 + cw] = _silu(_dot(hb, w_ref[:, 3 * d + c0:3 * d + c0 + cw])).astype(BF16)

    row = lax.broadcasted_iota(jnp.int32, (cw, cw), 0)
    col = lax.broadcasted_iota(jnp.int32, (cw, cw), 1)
    bd = jnp.where((row // DH_C) == (col // DH_C), 1.0, 0.0).astype(BF16)
    q_scale = -LOG2E * (DH_C ** -0.5)
    for cb in range(2 * d // cw):
        c0 = cb * cw
        t = raw_s[:, c0:c0 + cw]
        ss = _dot((t * t).astype(BF16), bd)
        tn = t * lax.rsqrt(ss * (1.0 / DH_C) + EPS)
        if c0 < d:
            q_o[:, c0:c0 + cw] = (tn * qg_ref[:, c0:c0 + cw] * q_scale).astype(BF16)
        else:
            k_o[:, c0 - d:c0 - d + cw] = (tn * kg_ref[:, c0 - d:c0 - d + cw]).astype(BF16)


def _inproj1(x, shift, scale, ln_g, w_in, q_norm, k_norm):
    bsz, s, d = x.shape
    tm = min(ROW_TILE, s)
    assert s % tm == 0 and d % COL_BLOCK == 0 and COL_BLOCK % DH_C == 0
    heads = d // DH_C
    vec = lambda: pl.BlockSpec((None, 1, d), lambda b, i: (b, 0, 0))
    tile = lambda: pl.BlockSpec((None, tm, d), lambda b, i: (b, i, 0))
    out = jax.ShapeDtypeStruct((bsz, s, d), BF16)
    return pl.pallas_call(
        _inproj1_body,
        grid=(bsz, s // tm),
        in_specs=[tile(), vec(), vec(), _const_spec((1, d)), _const_spec(w_in.shape),
                  _const_spec((1, d)), _const_spec((1, d))],
        out_specs=[tile(), tile(), tile(), tile()],
        out_shape=[out, out, out, out],
        scratch_shapes=[pltpu.VMEM((tm, d), BF16), pltpu.VMEM((tm, 2 * d), F32)],
        compiler_params=pltpu.CompilerParams(
            dimension_semantics=("arbitrary", "arbitrary"),
            vmem_limit_bytes=VMEM_LIMIT_BYTES),
        name="layer1_qkvz_proj",
    )(x, shift, scale, ln_g.reshape(1, d), w_in.astype(BF16),
      jnp.tile(q_norm, heads).reshape(1, d), jnp.tile(k_norm, heads).reshape(1, d))


PIPE_DEPTH = 8
NEG_BIG = -1e30


def _attn_body(q_ref, k_ref, v_ref, sz_ref, o_ref, qm_s, c_s, acc_s, zn_s, p_s, tot_s):
    s, width = q_ref.shape
    blk = ATTN_BLOCK
    nblk = s // blk
    heads = width // DH_C
    rows = heads * blk

    row = lax.broadcasted_iota(jnp.int32, (blk, blk), 0)
    col = lax.broadcasted_iota(jnp.int32, (blk, blk), 1)
    later = jnp.where(row > col, 1.0, 0.0).astype(BF16)
    causal = jnp.concatenate([col < row] * heads, axis=0)
    lane = lax.broadcasted_iota(jnp.int32, (blk, width), 1)

    def prep(i, _):
        q = q_ref[pl.ds(pl.multiple_of(i * blk, blk), blk), :]
        for hh in range(heads):
            keep = (lane >= hh * DH_C) & (lane < (hh + 1) * DH_C)
            qm_s[i, hh * blk:(hh + 1) * blk, :] = jnp.where(keep, q, jnp.zeros_like(q))
        c_s[i] = jnp.zeros((rows, width), F32)
        acc_s[i] = jnp.zeros((blk, width), F32)
        return 0

    lax.fori_loop(0, nblk, prep, 0)

    def scores(idx, slot):
        kb = k_ref[pl.ds(pl.multiple_of(idx[1] * blk, blk), blk), :]
        zn_s[slot] = lax.dot_general(qm_s[idx[0]], kb, (((1,), (1,)), ((), ())),
                                     preferred_element_type=F32)

    def gate_logs(slot, diag):
        zn = zn_s[slot]
        lm = jnp.minimum(zn, 0.0) - jnp.log2(1.0 + jnp.exp2(-jnp.abs(zn)))
        lb = lm - zn
        if diag:
            lm = jnp.where(causal, lm, 0.0)
        suf = _dot(lm.astype(BF16), later)
        p = lb + suf
        if diag:
            p = jnp.where(causal, p, NEG_BIG)
        p_s[slot] = p
        tot_s[slot] = jnp.broadcast_to(suf[:, 0:1] + lm[:, 0:1], (rows, width))

    def weigh(idx, slot):
        vb = v_ref[pl.ds(pl.multiple_of(idx[1] * blk, blk), blk), :]
        c = c_s[idx[0]]
        w = jnp.exp2(p_s[slot] + jnp.concatenate([c] * (blk // width), axis=1))
        pv = _dot(w.astype(BF16), vb)
        c_s[idx[0]] = c + tot_s[slot]
        out = pv[0:blk, :]
        for hh in range(1, heads):
            out = jnp.where(lane >= hh * DH_C, pv[hh * blk:(hh + 1) * blk, :], out)
        acc_s[idx[0]] += out

    def run_tiles(ntiles, first, step, diag):
        depth = PIPE_DEPTH
        if ntiles < 2 * depth:
            idx = first
            for _ in range(ntiles):
                scores(idx, 0)
                gate_logs(0, diag)
                weigh(idx, 0)
                idx = step(idx)
            return

        def one_step(win, n, do_weigh=True, do_logs=True, do_scores=True):
            slot = n % depth
            nxt = step(win[-1])
            if do_weigh:
                weigh(win[0], slot)
            if do_logs:
                gate_logs(slot, diag)
            if do_scores:
                scores(nxt, slot)
            return win[1:] + (nxt,)

        win = (first,) * (2 * depth)
        scores(first, 0)
        for n in range(1, 2 * depth):
            win = one_step(win, n, do_weigh=False, do_logs=n >= depth)
        steady = ntiles - 2 * depth
        n0 = 2 * depth
        for n in range(n0, n0 + steady % depth):
            win = one_step(win, n)
        n0 += steady % depth

        def body(_, win):
            for k in range(depth):
                win = one_step(win, n0 + k)
            return win

        win = lax.fori_loop(0, steady // depth, body, win)
        for n in range(ntiles, ntiles + 2 * depth):
            win = one_step(win, n, do_logs=n < ntiles + depth, do_scores=False)

    one = jnp.int32(1)
    zero = jnp.int32(0)
    run_tiles(nblk, (zero, zero), lambda ij: (ij[0] + 1, ij[1] + 1), True)
    run_tiles(nblk * (nblk - 1) // 2, (one, zero),
              lambda ij: (jnp.where(ij[1] > 0, ij[0], ij[0] + 1),
                          jnp.where(ij[1] > 0, ij[1] - 1, ij[0])), False)

    def finish(i, _):
        q0 = pl.multiple_of(i * blk, blk)
        o_ref[pl.ds(q0, blk), :] = (acc_s[i] * sz_ref[pl.ds(q0, blk), :].astype(F32)).astype(BF16)
        return 0

    lax.fori_loop(0, nblk, finish, 0)


def _attention(q, k, v, sz):
    bsz, s, d = q.shape
    assert s % ATTN_BLOCK == 0 and d % LANES == 0 and LANES % DH_C == 0
    nblk = s // ATTN_BLOCK
    heads = LANES // DH_C
    spec = lambda: pl.BlockSpec((None, s, LANES), lambda b, h: (b, 0, h))
    return pl.pallas_call(
        _attn_body,
        grid=(bsz, d // LANES),
        in_specs=[spec(), spec(), spec(), spec()],
        out_specs=spec(),
        out_shape=jax.ShapeDtypeStruct((bsz, s, d), BF16),
        scratch_shapes=[
            pltpu.VMEM((nblk, heads * ATTN_BLOCK, LANES), BF16),
            pltpu.VMEM((nblk, heads * ATTN_BLOCK, LANES), F32),
            pltpu.VMEM((nblk, ATTN_BLOCK, LANES), F32),
            pltpu.VMEM((PIPE_DEPTH, heads * ATTN_BLOCK, ATTN_BLOCK), F32),
            pltpu.VMEM((PIPE_DEPTH, heads * ATTN_BLOCK, ATTN_BLOCK), F32),
            pltpu.VMEM((PIPE_DEPTH, heads * ATTN_BLOCK, LANES), F32),
        ],
        compiler_params=pltpu.CompilerParams(
            dimension_semantics=("arbitrary", "arbitrary"),
            vmem_limit_bytes=VMEM_LIMIT_BYTES),
        name="stick_breaking_attention",
    )(q, k, v, sz)


def _outproj_body(y_ref, x_ref, gate_ref, w_ref, o_ref):
    o_ref[...] = x_ref[...] + gate_ref[...] * _dot(y_ref[...], w_ref[...])


def _outproj(y, x, gate, w_out):
    bsz, s, d = x.shape
    tm = min(ROW_TILE, s)
    assert s % tm == 0
    return pl.pallas_call(
        _outproj_body,
        grid=(bsz, s // tm),
        in_specs=[
            pl.BlockSpec((None, tm, y.shape[2]), lambda b, i: (b, i, 0)),
            pl.BlockSpec((None, tm, d), lambda b, i: (b, i, 0)),
            pl.BlockSpec((None, 1, d), lambda b, i: (b, 0, 0)),
            _const_spec(w_out.shape),
        ],
        out_specs=pl.BlockSpec((None, tm, d), lambda b, i: (b, i, 0)),
        out_shape=jax.ShapeDtypeStruct((bsz, s, d), F32),
        compiler_params=pltpu.CompilerParams(
            dimension_semantics=("arbitrary", "arbitrary"),
            vmem_limit_bytes=VMEM_LIMIT_BYTES),
        name="layer1_out_proj",
    )(y, x, gate, w_out.astype(BF16))


def kernel(x, c, ln_g, ada_w, ada_b, w_in_ab, conv_w, sg_norm, sg_w, sg_b, w_out_ab,
           w_in_c, q_norm, k_norm, w_out_c):
    bsz, _, d = x.shape
    depth = ada_w.shape[0]
    mod = _modulation(c, ada_w, ada_b)
    for l in range(depth):
        shift, scale, gate = (mod[l, :, j * d:(j + 1) * d].reshape(bsz, 1, d) for j in range(3))
        i = l // 2
        if l % 2 == 0:
            x = _layer0(x, shift, scale, gate, ln_g[l], w_in_ab[i], conv_w[i], sg_norm[i],
                        sg_w[i], sg_b[i], w_out_ab[i])
        else:
            q, k, v, sz = _inproj1(x, shift, scale, ln_g[l], w_in_c[i], q_norm[i], k_norm[i])
            y = _attention(q, k, v, sz)
            x = _outproj(y, x, gate, w_out_c[i])
    return x
```

```python
import functools

import jax
import jax.numpy as jnp
from jax import lax
from jax.experimental import pallas as pl
from jax.experimental.pallas import tpu as pltpu

F32 = jnp.float32
BF16 = jnp.bfloat16

EPS = 1e-6
CONV_W = 3
DH_B = 128
CHUNK = 128
DH_C = 64
LOG2E = 1.4426950408889634

LANES = 128
SUBLANES = 8
MXU_TILE = 256
VMEM_LIMIT_BYTES = 56 * 1024 * 1024

COL_BLOCK = MXU_TILE
ROW_TILE = 512
ATTN_BLOCK = 256


def _const_spec(shape):
    nd = len(shape)
    return pl.BlockSpec(shape, lambda *_: (0,) * nd, pipeline_mode=pl.Buffered(1))


def _dot(a, b):
    return jnp.dot(a, b, preferred_element_type=F32)


def _silu(x):
    return x * jax.nn.sigmoid(x)


def _mod_body(c_ref, w_ref, b_ref, o_ref):
    c = c_ref[...]
    o_ref[0] = jnp.dot(_silu(c), w_ref[0], preferred_element_type=F32,
                       precision=lax.Precision.HIGHEST) + b_ref[0]


def _modulation(c, ada_w, ada_b):
    depth, d, n3 = ada_w.shape
    bsz = c.shape[0]
    rows = -(-bsz // SUBLANES) * SUBLANES
    c_pad = jnp.pad(c, ((0, rows - bsz), (0, 0)))
    tn = d
    mod = pl.pallas_call(
        _mod_body,
        grid=(depth, n3 // tn),
        in_specs=[
            pl.BlockSpec((rows, d), lambda l, n: (0, 0)),
            pl.BlockSpec((1, d, tn), lambda l, n: (l, 0, n)),
            pl.BlockSpec((1, 1, tn), lambda l, n: (l, 0, n)),
        ],
        out_specs=pl.BlockSpec((1, rows, tn), lambda l, n: (l, 0, n)),
        out_shape=jax.ShapeDtypeStruct((depth, rows, n3), F32),
        compiler_params=pltpu.CompilerParams(
            dimension_semantics=("arbitrary", "arbitrary"),
            vmem_limit_bytes=VMEM_LIMIT_BYTES),
        name="adaln_modulation",
    )(c_pad, ada_w, ada_b.reshape(depth, 1, n3))
    return mod[:, :bsz, :]


def _modulated_norm(x, g, scale, shift):
    ms = jnp.mean(x * x, axis=-1, keepdims=True)
    xn = x * lax.rsqrt(ms + EPS)
    return (xn * g) * (1.0 + scale) + shift


def _layer0_body(x_ref, shift_ref, scale_ref, gate_ref, g_ref, win_ref, convw_ref,
                 sgn_ref, sgw_ref, sgb_ref, wout_ref, o_ref,
                 h_s, u_s, carry_s, y_s):
    tm, d = x_ref.shape
    d_a = convw_ref.shape[1]
    d_b = sgb_ref.shape[1]
    cw = COL_BLOCK

    @pl.when(pl.program_id(1) == 0)
    def _():
        carry_s[...] = jnp.zeros_like(carry_s)

    x = x_ref[...]
    h_s[...] = _modulated_norm(x, g_ref[...], scale_ref[...], shift_ref[...]).astype(BF16)
    hb = h_s[...]

    for cb in range(d_a // cw):
        c0 = cb * cw
        bg = _dot(hb, win_ref[:, c0:c0 + cw])
        cg = _dot(hb, win_ref[:, d_a + c0:d_a + c0 + cw])
        xa = _dot(hb, win_ref[:, 2 * d_a + c0:2 * d_a + c0 + cw])
        za = _dot(hb, win_ref[:, 3 * d_a + c0:3 * d_a + c0 + cw])
        u = cg * xa
        u_s[0:SUBLANES, :] = carry_s[:, c0:c0 + cw]
        u_s[SUBLANES:SUBLANES + tm, :] = u
        carry_s[:, c0:c0 + cw] = u[tm - SUBLANES:tm, :]
        u1 = u_s[SUBLANES - 1:SUBLANES - 1 + tm, :]
        u2 = u_s[SUBLANES - 2:SUBLANES - 2 + tm, :]
        conv = (convw_ref[2:3, c0:c0 + cw] * u + convw_ref[1:2, c0:c0 + cw] * u1
                + convw_ref[0:1, c0:c0 + cw] * u2)
        y_s[:, c0:c0 + cw] = (bg * conv * _silu(za)).astype(BF16)

    row = lax.broadcasted_iota(jnp.int32, (CHUNK, CHUNK), 0)
    col = lax.broadcasted_iota(jnp.int32, (CHUNK, CHUNK), 1)
    tril = row >= col
    off = 4 * d_a
    for cb in range(d_b // cw):
        c0 = cb * cw
        ub = _dot(hb, win_ref[:, off + c0:off + c0 + cw])
        vb = _dot(hb, win_ref[:, off + d_b + c0:off + d_b + c0 + cw])
        zb = _dot(hb, win_ref[:, off + 2 * d_b + c0:off + 2 * d_b + c0 + cw])
        gz = ub * _silu(zb)
        for gg in range(cw // DH_B):
            grp = (c0 + gg * DH_B) // DH_B
            l0 = gg * DH_B
            vg = vb[:, l0:l0 + DH_B]
            ms = jnp.mean(vg * vg, axis=-1, keepdims=True)
            vn = (vg * lax.rsqrt(ms + EPS) * sgn_ref[grp:grp + 1, :]).astype(BF16)
            wt = jnp.where(tril, sgw_ref[grp], 0.0).astype(BF16)
            bias = sgb_ref[:, c0 + l0:c0 + l0 + DH_B]
            for ch in range(tm // CHUNK):
                r0 = ch * CHUNK
                sgate = _dot(wt, vn[r0:r0 + CHUNK, :]) + bias
                y_s[r0:r0 + CHUNK, d_a + c0 + l0:d_a + c0 + l0 + DH_B] = (
                    gz[r0:r0 + CHUNK, l0:l0 + DH_B] * sgate).astype(BF16)

    out = _dot(y_s[...], wout_ref[...])
    o_ref[...] = x + gate_ref[...] * out


def _layer0(x, shift, scale, gate, ln_g, w_in, conv_w, sg_norm, sg_w, sg_b, w_out):
    bsz, s, d = x.shape
    d_a = conv_w.shape[1]
    g_b = sg_norm.shape[0]
    d_b = g_b * DH_B
    tm = min(ROW_TILE, s)
    assert s % tm == 0 and tm % CHUNK == 0 and d_a % COL_BLOCK == 0 and d_b % COL_BLOCK == 0
    bias_full = jnp.broadcast_to(sg_b.T[:, :, None], (CHUNK, g_b, DH_B)).reshape(CHUNK, d_b)
    vec = lambda: pl.BlockSpec((None, 1, d), lambda b, i: (b, 0, 0))
    return pl.pallas_call(
        _layer0_body,
        grid=(bsz, s // tm),
        in_specs=[
            pl.BlockSpec((None, tm, d), lambda b, i: (b, i, 0)),
            vec(), vec(), vec(),
            _const_spec((1, d)),
            _const_spec(w_in.shape),
            _const_spec(conv_w.shape),
            _const_spec(sg_norm.shape),
            _const_spec(sg_w.shape),
            _const_spec(bias_full.shape),
            _const_spec(w_out.shape),
        ],
        out_specs=pl.BlockSpec((None, tm, d), lambda b, i: (b, i, 0)),
        out_shape=jax.ShapeDtypeStruct((bsz, s, d), F32),
        scratch_shapes=[
            pltpu.VMEM((tm, d), BF16),
            pltpu.VMEM((tm + SUBLANES, COL_BLOCK), F32),
            pltpu.VMEM((SUBLANES, d_a), F32),
            pltpu.VMEM((tm, d_a + d_b), BF16),
        ],
        compiler_params=pltpu.CompilerParams(
            dimension_semantics=("arbitrary", "arbitrary"),
            vmem_limit_bytes=VMEM_LIMIT_BYTES),
        name="layer0_conv_sgmlp",
    )(x, shift, scale, gate, ln_g.reshape(1, d), w_in.astype(BF16), conv_w, sg_norm, sg_w,
      bias_full, w_out.astype(BF16))


def _inproj1_body(x_ref, shift_ref, scale_ref, g_ref, w_ref, qg_ref, kg_ref,
                  q_o, k_o, v_o, sz_o, h_s, raw_s):
    tm, d = x_ref.shape
    cw = COL_BLOCK
    h_s[...] = _modulated_norm(x_ref[...], g_ref[...], scale_ref[...], shift_ref[...]).astype(BF16)
    hb = h_s[...]
    for cb in range(2 * d // cw):
        raw_s[:, cb * cw:(cb + 1) * cw] = _dot(hb, w_ref[:, cb * cw:(cb + 1) * cw])
    for cb in range(d // cw):
        c0 = cb * cw
        v_o[:, c0:c0 + cw] = _dot(hb, w_ref[:, 2 * d + c0:2 * d + c0 + cw]).astype(BF16)
        sz_o[:, c0:c0 + cw] = _silu(_dot(hb, w_ref[:, 3 * d + c0:3 * d + c0 + cw])).astype(BF16)

    row = lax.broadcasted_iota(jnp.int32, (cw, cw), 0)
    col = lax.broadcasted_iota(jnp.int32, (cw, cw), 1)
    bd = jnp.where((row // DH_C) == (col // DH_C), 1.0, 0.0).astype(BF16)
    q_scale = LOG2E * (DH_C ** -0.5)
    for cb in range(2 * d // cw):
        c0 = cb * cw
        t = raw_s[:, c0:c0 + cw]
        ss = _dot((t * t).astype(BF16), bd)
        tn = t * lax.rsqrt(ss * (1.0 / DH_C) + EPS)
        if c0 < d:
            q_o[:, c0:c0 + cw] = (tn * qg_ref[:, c0:c0 + cw] * q_scale).astype(BF16)
        else:
            k_o[:, c0 - d:c0 - d + cw] = (tn * kg_ref[:, c0 - d:c0 - d + cw]).astype(BF16)


def _inproj1(x, shift, scale, ln_g, w_in, q_norm, k_norm):
    bsz, s, d = x.shape
    tm = min(ROW_TILE, s)
    assert s % tm == 0 and d % COL_BLOCK == 0 and COL_BLOCK % DH_C == 0
    heads = d // DH_C
    vec = lambda: pl.BlockSpec((None, 1, d), lambda b, i: (b, 0, 0))
    tile = lambda: pl.BlockSpec((None, tm, d), lambda b, i: (b, i, 0))
    out = jax.ShapeDtypeStruct((bsz, s, d), BF16)
    return pl.pallas_call(
        _inproj1_body,
        grid=(bsz, s // tm),
        in_specs=[tile(), vec(), vec(), _const_spec((1, d)), _const_spec(w_in.shape),
                  _const_spec((1, d)), _const_spec((1, d))],
        out_specs=[tile(), tile(), tile(), tile()],
        out_shape=[out, out, out, out],
        scratch_shapes=[pltpu.VMEM((tm, d), BF16), pltpu.VMEM((tm, 2 * d), F32)],
        compiler_params=pltpu.CompilerParams(
            dimension_semantics=("arbitrary", "arbitrary"),
            vmem_limit_bytes=VMEM_LIMIT_BYTES),
        name="layer1_qkvz_proj",
    )(x, shift, scale, ln_g.reshape(1, d), w_in.astype(BF16),
      jnp.tile(q_norm, heads).reshape(1, d), jnp.tile(k_norm, heads).reshape(1, d))


PIPE_DEPTH = 8
Z_MAX = 126.0
NEG_BIG = -1e30


def _attn_body(q_ref, k_ref, v_ref, sz_ref, o_ref, qm_s, c_s, acc_s, zn_s, p_s, tot_s):
    s, width = q_ref.shape
    blk = ATTN_BLOCK
    nblk = s // blk
    heads = width // DH_C
    rows = heads * blk

    row = lax.broadcasted_iota(jnp.int32, (blk, blk), 0)
    col = lax.broadcasted_iota(jnp.int32, (blk, blk), 1)
    later = jnp.where(row > col, 1.0, 0.0).astype(BF16)
    causal = jnp.concatenate([col < row] * heads, axis=0)
    lane = lax.broadcasted_iota(jnp.int32, (blk, width), 1)

    def prep(i, _):
        q = q_ref[pl.ds(pl.multiple_of(i * blk, blk), blk), :]
        for hh in range(heads):
            keep = (lane >= hh * DH_C) & (lane < (hh + 1) * DH_C)
            qm_s[i, hh * blk:(hh + 1) * blk, :] = jnp.where(keep, q, jnp.zeros_like(q))
        c_s[i] = jnp.zeros((rows, width), F32)
        acc_s[i] = jnp.zeros((blk, width), F32)
        return 0

    lax.fori_loop(0, nblk, prep, 0)

    def scores(idx, slot):
        kb = k_ref[pl.ds(pl.multiple_of(idx[1] * blk, blk), blk), :]
        z = lax.dot_general(qm_s[idx[0]], kb, (((1,), (1,)), ((), ())),
                            preferred_element_type=F32)
        zn_s[slot] = jnp.minimum(z, Z_MAX)

    def gate_logs(slot, diag):
        z = zn_s[slot]
        lm = jnp.log(1.0 + jnp.exp2(z)) * (-LOG2E)
        lb = lm + z
        if diag:
            lm = jnp.where(causal, lm, 0.0)
        suf = _dot(lm.astype(BF16), later)
        p = lb + suf
        if diag:
            p = jnp.where(causal, p, NEG_BIG)
        p_s[slot] = p
        tot_s[slot] = jnp.broadcast_to(suf[:, 0:1] + lm[:, 0:1], (rows, width))

    def weigh(idx, slot):
        vb = v_ref[pl.ds(pl.multiple_of(idx[1] * blk, blk), blk), :]
        c = c_s[idx[0]]
        w = jnp.exp2(p_s[slot] + jnp.concatenate([c] * (blk // width), axis=1))
        pv = _dot(w.astype(BF16), vb)
        c_s[idx[0]] = c + tot_s[slot]
        out = pv[0:blk, :]
        for hh in range(1, heads):
            out = jnp.where(lane >= hh * DH_C, pv[hh * blk:(hh + 1) * blk, :], out)
        acc_s[idx[0]] += out

    def run_tiles(ntiles, first, step, diag):
        depth = PIPE_DEPTH
        if ntiles < 2 * depth:
            idx = first
            for _ in range(ntiles):
                scores(idx, 0)
                gate_logs(0, diag)
                weigh(idx, 0)
                idx = step(idx)
            return

        def one_step(win, n, do_weigh=True, do_logs=True, do_scores=True):
            slot = n % depth
            nxt = step(win[-1])
            if do_weigh:
                weigh(win[0], slot)
            if do_logs:
                gate_logs(slot, diag)
            if do_scores:
                scores(nxt, slot)
            return win[1:] + (nxt,)

        win = (first,) * (2 * depth)
        scores(first, 0)
        for n in range(1, 2 * depth):
            win = one_step(win, n, do_weigh=False, do_logs=n >= depth)
        steady = ntiles - 2 * depth
        n0 = 2 * depth
        for n in range(n0, n0 + steady % depth):
            win = one_step(win, n)
        n0 += steady % depth

        def body(_, win):
            for k in range(depth):
                win = one_step(win, n0 + k)
            return win

        win = lax.fori_loop(0, steady // depth, body, win)
        for n in range(ntiles, ntiles + 2 * depth):
            win = one_step(win, n, do_logs=n < ntiles + depth, do_scores=False)

    one = jnp.int32(1)
    zero = jnp.int32(0)
    run_tiles(nblk, (zero, zero), lambda ij: (ij[0] + 1, ij[1] + 1), True)
    run_tiles(nblk * (nblk - 1) // 2, (one, zero),
              lambda ij: (jnp.where(ij[1] > 0, ij[0], ij[0] + 1),
                          jnp.where(ij[1] > 0, ij[1] - 1, ij[0])), False)

    def finish(i, _):
        q0 = pl.multiple_of(i * blk, blk)
        o_ref[pl.ds(q0, blk), :] = (acc_s[i] * sz_ref[pl.ds(q0, blk), :].astype(F32)).astype(BF16)
        return 0

    lax.fori_loop(0, nblk, finish, 0)


def _attention(q, k, v, sz):
    bsz, s, d = q.shape
    assert s % ATTN_BLOCK == 0 and d % LANES == 0 and LANES % DH_C == 0
    nblk = s // ATTN_BLOCK
    heads = LANES // DH_C
    spec = lambda: pl.BlockSpec((None, s, LANES), lambda b, h: (b, 0, h))
    return pl.pallas_call(
        _attn_body,
        grid=(bsz, d // LANES),
        in_specs=[spec(), spec(), spec(), spec()],
        out_specs=spec(),
        out_shape=jax.ShapeDtypeStruct((bsz, s, d), BF16),
        scratch_shapes=[
            pltpu.VMEM((nblk, heads * ATTN_BLOCK, LANES), BF16),
            pltpu.VMEM((nblk, heads * ATTN_BLOCK, LANES), F32),
            pltpu.VMEM((nblk, ATTN_BLOCK, LANES), F32),
            pltpu.VMEM((PIPE_DEPTH, heads * ATTN_BLOCK, ATTN_BLOCK), F32),
            pltpu.VMEM((PIPE_DEPTH, heads * ATTN_BLOCK, ATTN_BLOCK), F32),
            pltpu.VMEM((PIPE_DEPTH, heads * ATTN_BLOCK, LANES), F32),
        ],
        compiler_params=pltpu.CompilerParams(
            dimension_semantics=("arbitrary", "arbitrary"),
            vmem_limit_bytes=VMEM_LIMIT_BYTES),
        name="stick_breaking_attention",
    )(q, k, v, sz)


def _outproj_body(y_ref, x_ref, gate_ref, w_ref, o_ref):
    o_ref[...] = x_ref[...] + gate_ref[...] * _dot(y_ref[...], w_ref[...])


def _outproj(y, x, gate, w_out):
    bsz, s, d = x.shape
    tm = min(ROW_TILE, s)
    assert s % tm == 0
    return pl.pallas_call(
        _outproj_body,
        grid=(bsz, s // tm),
        in_specs=[
            pl.BlockSpec((None, tm, y.shape[2]), lambda b, i: (b, i, 0)),
            pl.BlockSpec((None, tm, d), lambda b, i: (b, i, 0)),
            pl.BlockSpec((None, 1, d), lambda b, i: (b, 0, 0)),
            _const_spec(w_out.shape),
        ],
        out_specs=pl.BlockSpec((None, tm, d), lambda b, i: (b, i, 0)),
        out_shape=jax.ShapeDtypeStruct((bsz, s, d), F32),
        compiler_params=pltpu.CompilerParams(
            dimension_semantics=("arbitrary", "arbitrary"),
            vmem_limit_bytes=VMEM_LIMIT_BYTES),
        name="layer1_out_proj",
    )(y, x, gate, w_out.astype(BF16))


def kernel(x, c, ln_g, ada_w, ada_b, w_in_ab, conv_w, sg_norm, sg_w, sg_b, w_out_ab,
           w_in_c, q_norm, k_norm, w_out_c):
    bsz, _, d = x.shape
    depth = ada_w.shape[0]
    mod = _modulation(c, ada_w, ada_b)
    for l in range(depth):
        shift, scale, gate = (mod[l, :, j * d:(j + 1) * d].reshape(bsz, 1, d) for j in range(3))
        i = l // 2
        if l % 2 == 0:
            x = _layer0(x, shift, scale, gate, ln_g[l], w_in_ab[i], conv_w[i], sg_norm[i],
                        sg_w[i], sg_b[i], w_out_ab[i])
        else:
            q, k, v, sz = _inproj1(x, shift, scale, ln_g[l], w_in_c[i], q_norm[i], k_norm[i])
            y = _attention(q, k, v, sz)
            x = _outproj(y, x, gate, w_out_c[i])
    return x
```

```python
import functools

import jax
import jax.numpy as jnp
from jax import lax
from jax.experimental import pallas as pl
from jax.experimental.pallas import tpu as pltpu

F32 = jnp.float32
BF16 = jnp.bfloat16

EPS = 1e-6
CONV_W = 3
DH_B = 128
CHUNK = 128
DH_C = 64
LOG2E = 1.4426950408889634

LANES = 128
SUBLANES = 8
MXU_TILE = 256
VMEM_LIMIT_BYTES = 56 * 1024 * 1024

COL_BLOCK = MXU_TILE
ROW_TILE = 512
ATTN_BLOCK = 256


def _const_spec(shape):
    nd = len(shape)
    return pl.BlockSpec(shape, lambda *_: (0,) * nd, pipeline_mode=pl.Buffered(1))


def _dot(a, b):
    return jnp.dot(a, b, preferred_element_type=F32)


def _silu(x):
    return x * jax.nn.sigmoid(x)


def _mod_body(c_ref, w_ref, b_ref, o_ref):
    c = c_ref[...]
    o_ref[0] = jnp.dot(_silu(c), w_ref[0], preferred_element_type=F32,
                       precision=lax.Precision.HIGHEST) + b_ref[0]


def _modulation(c, ada_w, ada_b):
    depth, d, n3 = ada_w.shape
    bsz = c.shape[0]
    rows = -(-bsz // SUBLANES) * SUBLANES
    c_pad = jnp.pad(c, ((0, rows - bsz), (0, 0)))
    tn = d
    mod = pl.pallas_call(
        _mod_body,
        grid=(depth, n3 // tn),
        in_specs=[
            pl.BlockSpec((rows, d), lambda l, n: (0, 0)),
            pl.BlockSpec((1, d, tn), lambda l, n: (l, 0, n)),
            pl.BlockSpec((1, 1, tn), lambda l, n: (l, 0, n)),
        ],
        out_specs=pl.BlockSpec((1, rows, tn), lambda l, n: (l, 0, n)),
        out_shape=jax.ShapeDtypeStruct((depth, rows, n3), F32),
        compiler_params=pltpu.CompilerParams(
            dimension_semantics=("arbitrary", "arbitrary"),
            vmem_limit_bytes=VMEM_LIMIT_BYTES),
        name="adaln_modulation",
    )(c_pad, ada_w, ada_b.reshape(depth, 1, n3))
    return mod[:, :bsz, :]


def _modulated_norm(x, g, scale, shift):
    ms = jnp.mean(x * x, axis=-1, keepdims=True)
    xn = x * lax.rsqrt(ms + EPS)
    return (xn * g) * (1.0 + scale) + shift


def _layer0_body(x_ref, shift_ref, scale_ref, gate_ref, g_ref, win_ref, convw_ref,
                 sgn_ref, sgw_ref, sgb_ref, wout_ref, o_ref,
                 h_s, u_s, carry_s, y_s):
    tm, d = x_ref.shape
    d_a = convw_ref.shape[1]
    d_b = sgb_ref.shape[1]
    cw = COL_BLOCK

    @pl.when(pl.program_id(1) == 0)
    def _():
        carry_s[...] = jnp.zeros_like(carry_s)

    x = x_ref[...]
    h_s[...] = _modulated_norm(x, g_ref[...], scale_ref[...], shift_ref[...]).astype(BF16)
    hb = h_s[...]

    for cb in range(d_a // cw):
        c0 = cb * cw
        bg = _dot(hb, win_ref[:, c0:c0 + cw])
        cg = _dot(hb, win_ref[:, d_a + c0:d_a + c0 + cw])
        xa = _dot(hb, win_ref[:, 2 * d_a + c0:2 * d_a + c0 + cw])
        za = _dot(hb, win_ref[:, 3 * d_a + c0:3 * d_a + c0 + cw])
        u = cg * xa
        u_s[0:SUBLANES, :] = carry_s[:, c0:c0 + cw]
        u_s[SUBLANES:SUBLANES + tm, :] = u
        carry_s[:, c0:c0 + cw] = u[tm - SUBLANES:tm, :]
        u1 = u_s[SUBLANES - 1:SUBLANES - 1 + tm, :]
        u2 = u_s[SUBLANES - 2:SUBLANES - 2 + tm, :]
        conv = (convw_ref[2:3, c0:c0 + cw] * u + convw_ref[1:2, c0:c0 + cw] * u1
                + convw_ref[0:1, c0:c0 + cw] * u2)
        y_s[:, c0:c0 + cw] = (bg * conv * _silu(za)).astype(BF16)

    row = lax.broadcasted_iota(jnp.int32, (CHUNK, CHUNK), 0)
    col = lax.broadcasted_iota(jnp.int32, (CHUNK, CHUNK), 1)
    tril = row >= col
    off = 4 * d_a
    for cb in range(d_b // cw):
        c0 = cb * cw
        ub = _dot(hb, win_ref[:, off + c0:off + c0 + cw])
        vb = _dot(hb, win_ref[:, off + d_b + c0:off + d_b + c0 + cw])
        zb = _dot(hb, win_ref[:, off + 2 * d_b + c0:off + 2 * d_b + c0 + cw])
        gz = ub * _silu(zb)
        for gg in range(cw // DH_B):
            grp = (c0 + gg * DH_B) // DH_B
            l0 = gg * DH_B
            vg = vb[:, l0:l0 + DH_B]
            ms = jnp.mean(vg * vg, axis=-1, keepdims=True)
            vn = (vg * lax.rsqrt(ms + EPS) * sgn_ref[grp:grp + 1, :]).astype(BF16)
            wt = jnp.where(tril, sgw_ref[grp], 0.0).astype(BF16)
            bias = sgb_ref[:, c0 + l0:c0 + l0 + DH_B]
            for ch in range(tm // CHUNK):
                r0 = ch * CHUNK
                sgate = _dot(wt, vn[r0:r0 + CHUNK, :]) + bias
                y_s[r0:r0 + CHUNK, d_a + c0 + l0:d_a + c0 + l0 + DH_B] = (
                    gz[r0:r0 + CHUNK, l0:l0 + DH_B] * sgate).astype(BF16)

    out = _dot(y_s[...], wout_ref[...])
    o_ref[...] = x + gate_ref[...] * out


def _layer0(x, shift, scale, gate, ln_g, w_in, conv_w, sg_norm, sg_w, sg_b, w_out):
    bsz, s, d = x.shape
    d_a = conv_w.shape[1]
    g_b = sg_norm.shape[0]
    d_b = g_b * DH_B
    tm = min(ROW_TILE, s)
    assert s % tm == 0 and tm % CHUNK == 0 and d_a % COL_BLOCK == 0 and d_b % COL_BLOCK == 0
    bias_full = jnp.broadcast_to(sg_b.T[:, :, None], (CHUNK, g_b, DH_B)).reshape(CHUNK, d_b)
    vec = lambda: pl.BlockSpec((None, 1, d), lambda b, i: (b, 0, 0))
    return pl.pallas_call(
        _layer0_body,
        grid=(bsz, s // tm),
        in_specs=[
            pl.BlockSpec((None, tm, d), lambda b, i: (b, i, 0)),
            vec(), vec(), vec(),
            _const_spec((1, d)),
            _const_spec(w_in.shape),
            _const_spec(conv_w.shape),
            _const_spec(sg_norm.shape),
            _const_spec(sg_w.shape),
            _const_spec(bias_full.shape),
            _const_spec(w_out.shape),
        ],
        out_specs=pl.BlockSpec((None, tm, d), lambda b, i: (b, i, 0)),
        out_shape=jax.ShapeDtypeStruct((bsz, s, d), F32),
        scratch_shapes=[
            pltpu.VMEM((tm, d), BF16),
            pltpu.VMEM((tm + SUBLANES, COL_BLOCK), F32),
            pltpu.VMEM((SUBLANES, d_a), F32),
            pltpu.VMEM((tm, d_a + d_b), BF16),
        ],
        compiler_params=pltpu.CompilerParams(
            dimension_semantics=("arbitrary", "arbitrary"),
            vmem_limit_bytes=VMEM_LIMIT_BYTES),
        name="layer0_conv_sgmlp",
    )(x, shift, scale, gate, ln_g.reshape(1, d), w_in.astype(BF16), conv_w, sg_norm, sg_w,
      bias_full, w_out.astype(BF16))


def _inproj1_body(x_ref, shift_ref, scale_ref, g_ref, w_ref, qg_ref, kg_ref,
                  q_o, k_o, vt_o, sz_o, h_s, raw_s):
    tm, d = x_ref.shape
    cw = COL_BLOCK
    h_s[...] = _modulated_norm(x_ref[...], g_ref[...], scale_ref[...], shift_ref[...]).astype(BF16)
    hb = h_s[...]
    for cb in range(2 * d // cw):
        raw_s[:, cb * cw:(cb + 1) * cw] = _dot(hb, w_ref[:, cb * cw:(cb + 1) * cw])
    for cb in range(d // cw):
        c0 = cb * cw
        v = _dot(hb, w_ref[:, 2 * d + c0:2 * d + c0 + cw])
        for kb in range(tm // ATTN_BLOCK):
            for hp in range(cw // LANES):
                vt_o[c0 // LANES + hp, kb] = v[kb * ATTN_BLOCK:(kb + 1) * ATTN_BLOCK,
                                              hp * LANES:(hp + 1) * LANES].T.astype(BF16)
        sz_o[:, c0:c0 + cw] = _silu(_dot(hb, w_ref[:, 3 * d + c0:3 * d + c0 + cw])).astype(BF16)

    row = lax.broadcasted_iota(jnp.int32, (cw, cw), 0)
    col = lax.broadcasted_iota(jnp.int32, (cw, cw), 1)
    bd = jnp.where((row // DH_C) == (col // DH_C), 1.0, 0.0).astype(BF16)
    q_scale = LOG2E * (DH_C ** -0.5)
    for cb in range(2 * d // cw):
        c0 = cb * cw
        t = raw_s[:, c0:c0 + cw]
        ss = _dot((t * t).astype(BF16), bd)
        tn = t * lax.rsqrt(ss * (1.0 / DH_C) + EPS)
        if c0 < d:
            q_o[:, c0:c0 + cw] = (tn * qg_ref[:, c0:c0 + cw] * q_scale).astype(BF16)
        else:
            k_o[:, c0 - d:c0 - d + cw] = (tn * kg_ref[:, c0 - d:c0 - d + cw]).astype(BF16)


def _inproj1(x, shift, scale, ln_g, w_in, q_norm, k_norm):
    bsz, s, d = x.shape
    tm = min(ROW_TILE, s)
    assert s % tm == 0 and d % COL_BLOCK == 0 and COL_BLOCK % DH_C == 0
    assert tm % ATTN_BLOCK == 0 and COL_BLOCK % LANES == 0
    heads = d // DH_C
    vec = lambda: pl.BlockSpec((None, 1, d), lambda b, i: (b, 0, 0))
    tile = lambda: pl.BlockSpec((None, tm, d), lambda b, i: (b, i, 0))
    out = jax.ShapeDtypeStruct((bsz, s, d), BF16)
    return pl.pallas_call(
        _inproj1_body,
        grid=(bsz, s // tm),
        in_specs=[tile(), vec(), vec(), _const_spec((1, d)), _const_spec(w_in.shape),
                  _const_spec((1, d)), _const_spec((1, d))],
        out_specs=[tile(), tile(),
                   pl.BlockSpec((None, d // LANES, tm // ATTN_BLOCK, LANES, ATTN_BLOCK),
                                lambda b, i: (b, 0, i, 0, 0)),
                   tile()],
        out_shape=[out, out,
                   jax.ShapeDtypeStruct((bsz, d // LANES, s // ATTN_BLOCK, LANES, ATTN_BLOCK), BF16),
                   out],
        scratch_shapes=[pltpu.VMEM((tm, d), BF16), pltpu.VMEM((tm, 2 * d), F32)],
        compiler_params=pltpu.CompilerParams(
            dimension_semantics=("arbitrary", "arbitrary"),
            vmem_limit_bytes=VMEM_LIMIT_BYTES),
        name="layer1_qkvz_proj",
    )(x, shift, scale, ln_g.reshape(1, d), w_in.astype(BF16),
      jnp.tile(q_norm, heads).reshape(1, d), jnp.tile(k_norm, heads).reshape(1, d))


PIPE_DEPTH = 8
Z_MAX = 126.0
NEG_BIG = -1e30


def _attn_body(q_ref, k_ref, vt_ref, sz_ref, o_ref, qm_s, c_s, acc_s, zn_s, p_s, tot_s):
    s, width = q_ref.shape
    blk = ATTN_BLOCK
    nblk = s // blk
    heads = width // DH_C
    rows = heads * blk

    row = lax.broadcasted_iota(jnp.int32, (blk, blk), 0)
    col = lax.broadcasted_iota(jnp.int32, (blk, blk), 1)
    later = jnp.where(row > col, 1.0, 0.0).astype(BF16)
    causal = jnp.concatenate([col < row] * heads, axis=0)
    lane = lax.broadcasted_iota(jnp.int32, (blk, width), 1)

    def prep(i, _):
        q = q_ref[pl.ds(pl.multiple_of(i * blk, blk), blk), :]
        for hh in range(heads):
            keep = (lane >= hh * DH_C) & (lane < (hh + 1) * DH_C)
            qm_s[i, hh * blk:(hh + 1) * blk, :] = jnp.where(keep, q, jnp.zeros_like(q))
        c_s[i] = jnp.zeros((rows, width), F32)
        acc_s[i] = jnp.zeros((width, blk), F32)
        return 0

    lax.fori_loop(0, nblk, prep, 0)

    def scores(idx, slot):
        kb = k_ref[pl.ds(pl.multiple_of(idx[1] * blk, blk), blk), :]
        z = lax.dot_general(qm_s[idx[0]], kb, (((1,), (1,)), ((), ())),
                            preferred_element_type=F32)
        zn_s[slot] = jnp.minimum(z, Z_MAX)

    def gate_logs(slot, diag):
        z = zn_s[slot]
        lm = jnp.log(1.0 + jnp.exp2(z)) * (-LOG2E)
        lb = lm + z
        if diag:
            lm = jnp.where(causal, lm, 0.0)
        suf = _dot(lm.astype(BF16), later)
        p = lb + suf
        if diag:
            p = jnp.where(causal, p, NEG_BIG)
        p_s[slot] = p
        tot_s[slot] = jnp.broadcast_to(suf[:, 0:1] + lm[:, 0:1], (rows, width))

    def weigh(idx, slot):
        c = c_s[idx[0]]
        w = jnp.exp2(p_s[slot] + jnp.concatenate([c] * (blk // width), axis=1))
        pv = lax.dot_general(vt_ref[idx[1]], w.astype(BF16), (((1,), (1,)), ((), ())),
                             preferred_element_type=F32)
        c_s[idx[0]] = c + tot_s[slot]
        for hh in range(heads):
            acc_s[idx[0], hh * DH_C:(hh + 1) * DH_C, :] += pv[hh * DH_C:(hh + 1) * DH_C,
                                                              hh * blk:(hh + 1) * blk]

    def run_tiles(ntiles, first, step, diag):
        depth = PIPE_DEPTH
        if ntiles < 2 * depth:
            idx = first
            for _ in range(ntiles):
                scores(idx, 0)
                gate_logs(0, diag)
                weigh(idx, 0)
                idx = step(idx)
            return

        def one_step(win, n, do_weigh=True, do_logs=True, do_scores=True):
            slot = n % depth
            nxt = step(win[-1])
            if do_weigh:
                weigh(win[0], slot)
            if do_logs:
                gate_logs(slot, diag)
            if do_scores:
                scores(nxt, slot)
            return win[1:] + (nxt,)

        win = (first,) * (2 * depth)
        scores(first, 0)
        for n in range(1, 2 * depth):
            win = one_step(win, n, do_weigh=False, do_logs=n >= depth)
        steady = ntiles - 2 * depth
        n0 = 2 * depth
        for n in range(n0, n0 + steady % depth):
            win = one_step(win, n)
        n0 += steady % depth

        def body(_, win):
            for k in range(depth):
                win = one_step(win, n0 + k)
            return win

        win = lax.fori_loop(0, steady // depth, body, win)
        for n in range(ntiles, ntiles + 2 * depth):
            win = one_step(win, n, do_logs=n < ntiles + depth, do_scores=False)

    one = jnp.int32(1)
    zero = jnp.int32(0)
    run_tiles(nblk, (zero, zero), lambda ij: (ij[0] + 1, ij[1] + 1), True)
    run_tiles(nblk * (nblk - 1) // 2, (one, zero),
              lambda ij: (jnp.where(ij[1] > 0, ij[0], ij[0] + 1),
                          jnp.where(ij[1] > 0, ij[1] - 1, ij[0])), False)

    def finish(i, _):
        q0 = pl.multiple_of(i * blk, blk)
        o_ref[pl.ds(q0, blk), :] = (acc_s[i].T * sz_ref[pl.ds(q0, blk), :].astype(F32)).astype(BF16)
        return 0

    lax.fori_loop(0, nblk, finish, 0)


def _attention(q, k, vt, sz):
    bsz, s, d = q.shape
    assert s % ATTN_BLOCK == 0 and d % LANES == 0 and LANES % DH_C == 0
    nblk = s // ATTN_BLOCK
    heads = LANES // DH_C
    spec = lambda: pl.BlockSpec((None, s, LANES), lambda b, h: (b, 0, h))
    return pl.pallas_call(
        _attn_body,
        grid=(bsz, d // LANES),
        in_specs=[spec(), spec(),
                  pl.BlockSpec((None, None, nblk, LANES, ATTN_BLOCK), lambda b, h: (b, h, 0, 0, 0)),
                  spec()],
        out_specs=spec(),
        out_shape=jax.ShapeDtypeStruct((bsz, s, d), BF16),
        scratch_shapes=[
            pltpu.VMEM((nblk, heads * ATTN_BLOCK, LANES), BF16),
            pltpu.VMEM((nblk, heads * ATTN_BLOCK, LANES), F32),
            pltpu.VMEM((nblk, LANES, ATTN_BLOCK), F32),
            pltpu.VMEM((PIPE_DEPTH, heads * ATTN_BLOCK, ATTN_BLOCK), F32),
            pltpu.VMEM((PIPE_DEPTH, heads * ATTN_BLOCK, ATTN_BLOCK), F32),
            pltpu.VMEM((PIPE_DEPTH, heads * ATTN_BLOCK, LANES), F32),
        ],
        compiler_params=pltpu.CompilerParams(
            dimension_semantics=("arbitrary", "arbitrary"),
            vmem_limit_bytes=VMEM_LIMIT_BYTES),
        name="stick_breaking_attention",
    )(q, k, vt, sz)


def _outproj_body(y_ref, x_ref, gate_ref, w_ref, o_ref):
    o_ref[...] = x_ref[...] + gate_ref[...] * _dot(y_ref[...], w_ref[...])


def _outproj(y, x, gate, w_out):
    bsz, s, d = x.shape
    tm = min(ROW_TILE, s)
    assert s % tm == 0
    return pl.pallas_call(
        _outproj_body,
        grid=(bsz, s // tm),
        in_specs=[
            pl.BlockSpec((None, tm, y.shape[2]), lambda b, i: (b, i, 0)),
            pl.BlockSpec((None, tm, d), lambda b, i: (b, i, 0)),
            pl.BlockSpec((None, 1, d), lambda b, i: (b, 0, 0)),
            _const_spec(w_out.shape),
        ],
        out_specs=pl.BlockSpec((None, tm, d), lambda b, i: (b, i, 0)),
        out_shape=jax.ShapeDtypeStruct((bsz, s, d), F32),
        compiler_params=pltpu.CompilerParams(
            dimension_semantics=("arbitrary", "arbitrary"),
            vmem_limit_bytes=VMEM_LIMIT_BYTES),
        name="layer1_out_proj",
    )(y, x, gate, w_out.astype(BF16))


def kernel(x, c, ln_g, ada_w, ada_b, w_in_ab, conv_w, sg_norm, sg_w, sg_b, w_out_ab,
           w_in_c, q_norm, k_norm, w_out_c):
    bsz, _, d = x.shape
    depth = ada_w.shape[0]
    mod = _modulation(c, ada_w, ada_b)
    for l in range(depth):
        shift, scale, gate = (mod[l, :, j * d:(j + 1) * d].reshape(bsz, 1, d) for j in range(3))
        i = l // 2
        if l % 2 == 0:
            x = _layer0(x, shift, scale, gate, ln_g[l], w_in_ab[i], conv_w[i], sg_norm[i],
                        sg_w[i], sg_b[i], w_out_ab[i])
        else:
            q, k, vt, sz = _inproj1(x, shift, scale, ln_g[l], w_in_c[i], q_norm[i], k_norm[i])
            y = _attention(q, k, vt, sz)
            x = _outproj(y, x, gate, w_out_c[i])
    return x
```

```python
import functools

import jax
import jax.numpy as jnp
from jax import lax
from jax.experimental import pallas as pl
from jax.experimental.pallas import tpu as pltpu

F32 = jnp.float32
BF16 = jnp.bfloat16

EPS = 1e-6
CONV_W = 3
DH_B = 128
CHUNK = 128
DH_C = 64
LOG2E = 1.4426950408889634

LANES = 128
SUBLANES = 8
MXU_TILE = 256
VMEM_LIMIT_BYTES = 56 * 1024 * 1024

COL_BLOCK = MXU_TILE
ROW_TILE = 512
ATTN_BLOCK = 256


def _const_spec(shape):
    nd = len(shape)
    return pl.BlockSpec(shape, lambda *_: (0,) * nd, pipeline_mode=pl.Buffered(1))


def _dot(a, b):
    return jnp.dot(a, b, preferred_element_type=F32)


def _silu(x):
    return x * jax.nn.sigmoid(x)


def _mod_body(c_ref, w_ref, b_ref, o_ref):
    c = c_ref[...]
    o_ref[0] = jnp.dot(_silu(c), w_ref[0], preferred_element_type=F32,
                       precision=lax.Precision.HIGHEST) + b_ref[0]


def _modulation(c, ada_w, ada_b):
    depth, d, n3 = ada_w.shape
    bsz = c.shape[0]
    rows = -(-bsz // SUBLANES) * SUBLANES
    c_pad = jnp.pad(c, ((0, rows - bsz), (0, 0)))
    tn = d
    mod = pl.pallas_call(
        _mod_body,
        grid=(depth, n3 // tn),
        in_specs=[
            pl.BlockSpec((rows, d), lambda l, n: (0, 0)),
            pl.BlockSpec((1, d, tn), lambda l, n: (l, 0, n)),
            pl.BlockSpec((1, 1, tn), lambda l, n: (l, 0, n)),
        ],
        out_specs=pl.BlockSpec((1, rows, tn), lambda l, n: (l, 0, n)),
        out_shape=jax.ShapeDtypeStruct((depth, rows, n3), F32),
        compiler_params=pltpu.CompilerParams(
            dimension_semantics=("arbitrary", "arbitrary"),
            vmem_limit_bytes=VMEM_LIMIT_BYTES),
        name="adaln_modulation",
    )(c_pad, ada_w, ada_b.reshape(depth, 1, n3))
    return mod[:, :bsz, :]


def _modulated_norm(x, g, scale, shift):
    ms = jnp.mean(x * x, axis=-1, keepdims=True)
    xn = x * lax.rsqrt(ms + EPS)
    return (xn * g) * (1.0 + scale) + shift


def _layer0_body(x_ref, shift_ref, scale_ref, gate_ref, g_ref, win_ref, convw_ref,
                 sgn_ref, sgw_ref, sgb_ref, wout_ref, o_ref,
                 h_s, u_s, carry_s, y_s):
    tm, d = x_ref.shape
    d_a = convw_ref.shape[1]
    d_b = sgb_ref.shape[1]
    cw = COL_BLOCK

    @pl.when(pl.program_id(1) == 0)
    def _():
        carry_s[...] = jnp.zeros_like(carry_s)

    x = x_ref[...]
    h_s[...] = _modulated_norm(x, g_ref[...], scale_ref[...], shift_ref[...]).astype(BF16)
    hb = h_s[...]

    for cb in range(d_a // cw):
        c0 = cb * cw
        bg = _dot(hb, win_ref[:, c0:c0 + cw])
        cg = _dot(hb, win_ref[:, d_a + c0:d_a + c0 + cw])
        xa = _dot(hb, win_ref[:, 2 * d_a + c0:2 * d_a + c0 + cw])
        za = _dot(hb, win_ref[:, 3 * d_a + c0:3 * d_a + c0 + cw])
        u = cg * xa
        u_s[0:SUBLANES, :] = carry_s[:, c0:c0 + cw]
        u_s[SUBLANES:SUBLANES + tm, :] = u
        carry_s[:, c0:c0 + cw] = u[tm - SUBLANES:tm, :]
        u1 = u_s[SUBLANES - 1:SUBLANES - 1 + tm, :]
        u2 = u_s[SUBLANES - 2:SUBLANES - 2 + tm, :]
        conv = (convw_ref[2:3, c0:c0 + cw] * u + convw_ref[1:2, c0:c0 + cw] * u1
                + convw_ref[0:1, c0:c0 + cw] * u2)
        y_s[:, c0:c0 + cw] = (bg * conv * _silu(za)).astype(BF16)

    row = lax.broadcasted_iota(jnp.int32, (CHUNK, CHUNK), 0)
    col = lax.broadcasted_iota(jnp.int32, (CHUNK, CHUNK), 1)
    tril = row >= col
    off = 4 * d_a
    for cb in range(d_b // cw):
        c0 = cb * cw
        ub = _dot(hb, win_ref[:, off + c0:off + c0 + cw])
        vb = _dot(hb, win_ref[:, off + d_b + c0:off + d_b + c0 + cw])
        zb = _dot(hb, win_ref[:, off + 2 * d_b + c0:off + 2 * d_b + c0 + cw])
        gz = ub * _silu(zb)
        for gg in range(cw // DH_B):
            grp = (c0 + gg * DH_B) // DH_B
            l0 = gg * DH_B
            vg = vb[:, l0:l0 + DH_B]
            ms = jnp.mean(vg * vg, axis=-1, keepdims=True)
            vn = (vg * lax.rsqrt(ms + EPS) * sgn_ref[grp:grp + 1, :]).astype(BF16)
            wt = jnp.where(tril, sgw_ref[grp], 0.0).astype(BF16)
            bias = sgb_ref[:, c0 + l0:c0 + l0 + DH_B]
            for ch in range(tm // CHUNK):
                r0 = ch * CHUNK
                sgate = _dot(wt, vn[r0:r0 + CHUNK, :]) + bias
                y_s[r0:r0 + CHUNK, d_a + c0 + l0:d_a + c0 + l0 + DH_B] = (
                    gz[r0:r0 + CHUNK, l0:l0 + DH_B] * sgate).astype(BF16)

    out = _dot(y_s[...], wout_ref[...])
    o_ref[...] = x + gate_ref[...] * out


def _layer0(x, shift, scale, gate, ln_g, w_in, conv_w, sg_norm, sg_w, sg_b, w_out):
    bsz, s, d = x.shape
    d_a = conv_w.shape[1]
    g_b = sg_norm.shape[0]
    d_b = g_b * DH_B
    tm = min(ROW_TILE, s)
    assert s % tm == 0 and tm % CHUNK == 0 and d_a % COL_BLOCK == 0 and d_b % COL_BLOCK == 0
    bias_full = jnp.broadcast_to(sg_b.T[:, :, None], (CHUNK, g_b, DH_B)).reshape(CHUNK, d_b)
    vec = lambda: pl.BlockSpec((None, 1, d), lambda b, i: (b, 0, 0))
    return pl.pallas_call(
        _layer0_body,
        grid=(bsz, s // tm),
        in_specs=[
            pl.BlockSpec((None, tm, d), lambda b, i: (b, i, 0)),
            vec(), vec(), vec(),
            _const_spec((1, d)),
            _const_spec(w_in.shape),
            _const_spec(conv_w.shape),
            _const_spec(sg_norm.shape),
            _const_spec(sg_w.shape),
            _const_spec(bias_full.shape),
            _const_spec(w_out.shape),
        ],
        out_specs=pl.BlockSpec((None, tm, d), lambda b, i: (b, i, 0)),
        out_shape=jax.ShapeDtypeStruct((bsz, s, d), F32),
        scratch_shapes=[
            pltpu.VMEM((tm, d), BF16),
            pltpu.VMEM((tm + SUBLANES, COL_BLOCK), F32),
            pltpu.VMEM((SUBLANES, d_a), F32),
            pltpu.VMEM((tm, d_a + d_b), BF16),
        ],
        compiler_params=pltpu.CompilerParams(
            dimension_semantics=("arbitrary", "arbitrary"),
            vmem_limit_bytes=VMEM_LIMIT_BYTES),
        name="layer0_conv_sgmlp",
    )(x, shift, scale, gate, ln_g.reshape(1, d), w_in.astype(BF16), conv_w, sg_norm, sg_w,
      bias_full, w_out.astype(BF16))


def _inproj1_body(x_ref, shift_ref, scale_ref, g_ref, w_ref, qg_ref, kg_ref,
                  q_o, kt_o, vt_o, sz_o, h_s, raw_s):
    tm, d = x_ref.shape
    cw = COL_BLOCK
    h_s[...] = _modulated_norm(x_ref[...], g_ref[...], scale_ref[...], shift_ref[...]).astype(BF16)
    hb = h_s[...]
    for cb in range(2 * d // cw):
        raw_s[:, cb * cw:(cb + 1) * cw] = _dot(hb, w_ref[:, cb * cw:(cb + 1) * cw])
    for cb in range(d // cw):
        c0 = cb * cw
        v = _dot(hb, w_ref[:, 2 * d + c0:2 * d + c0 + cw])
        for kb in range(tm // ATTN_BLOCK):
            for hp in range(cw // LANES):
                vt_o[c0 // LANES + hp, kb] = v[kb * ATTN_BLOCK:(kb + 1) * ATTN_BLOCK,
                                              hp * LANES:(hp + 1) * LANES].T.astype(BF16)
        sz_o[:, c0:c0 + cw] = _silu(_dot(hb, w_ref[:, 3 * d + c0:3 * d + c0 + cw])).astype(BF16)

    row = lax.broadcasted_iota(jnp.int32, (cw, cw), 0)
    col = lax.broadcasted_iota(jnp.int32, (cw, cw), 1)
    bd = jnp.where((row // DH_C) == (col // DH_C), 1.0, 0.0).astype(BF16)
    q_scale = LOG2E * (DH_C ** -0.5)
    for cb in range(2 * d // cw):
        c0 = cb * cw
        t = raw_s[:, c0:c0 + cw]
        ss = _dot((t * t).astype(BF16), bd)
        tn = t * lax.rsqrt(ss * (1.0 / DH_C) + EPS)
        if c0 < d:
            q_o[:, c0:c0 + cw] = (tn * qg_ref[:, c0:c0 + cw] * q_scale).astype(BF16)
        else:
            kn = tn * kg_ref[:, c0 - d:c0 - d + cw]
            for kb in range(tm // ATTN_BLOCK):
                for hp in range(cw // LANES):
                    kt_o[(c0 - d) // LANES + hp, kb] = kn[kb * ATTN_BLOCK:(kb + 1) * ATTN_BLOCK,
                                                          hp * LANES:(hp + 1) * LANES].T.astype(BF16)


def _inproj1(x, shift, scale, ln_g, w_in, q_norm, k_norm):
    bsz, s, d = x.shape
    tm = min(ROW_TILE, s)
    assert s % tm == 0 and d % COL_BLOCK == 0 and COL_BLOCK % DH_C == 0
    assert tm % ATTN_BLOCK == 0 and COL_BLOCK % LANES == 0
    heads = d // DH_C
    vec = lambda: pl.BlockSpec((None, 1, d), lambda b, i: (b, 0, 0))
    tile = lambda: pl.BlockSpec((None, tm, d), lambda b, i: (b, i, 0))
    out = jax.ShapeDtypeStruct((bsz, s, d), BF16)
    tposed = lambda: pl.BlockSpec((None, d // LANES, tm // ATTN_BLOCK, LANES, ATTN_BLOCK),
                                  lambda b, i: (b, 0, i, 0, 0))
    out_t = jax.ShapeDtypeStruct((bsz, d // LANES, s // ATTN_BLOCK, LANES, ATTN_BLOCK), BF16)
    return pl.pallas_call(
        _inproj1_body,
        grid=(bsz, s // tm),
        in_specs=[tile(), vec(), vec(), _const_spec((1, d)), _const_spec(w_in.shape),
                  _const_spec((1, d)), _const_spec((1, d))],
        out_specs=[tile(), tposed(), tposed(), tile()],
        out_shape=[out, out_t, out_t, out],
        scratch_shapes=[pltpu.VMEM((tm, d), BF16), pltpu.VMEM((tm, 2 * d), F32)],
        compiler_params=pltpu.CompilerParams(
            dimension_semantics=("arbitrary", "arbitrary"),
            vmem_limit_bytes=VMEM_LIMIT_BYTES),
        name="layer1_qkvz_proj",
    )(x, shift, scale, ln_g.reshape(1, d), w_in.astype(BF16),
      jnp.tile(q_norm, heads).reshape(1, d), jnp.tile(k_norm, heads).reshape(1, d))


PIPE_DEPTH = 8
Z_MAX = 126.0
NEG_BIG = -1e30


def _attn_body(q_ref, kt_ref, vt_ref, sz_ref, o_ref, qm_s, c_s, acc_s, zn_s, p_s, tot_s):
    s, width = q_ref.shape
    blk = ATTN_BLOCK
    nblk = s // blk
    heads = width // DH_C
    rows = heads * blk

    row = lax.broadcasted_iota(jnp.int32, (blk, blk), 0)
    col = lax.broadcasted_iota(jnp.int32, (blk, blk), 1)
    later = jnp.where(row > col, 1.0, 0.0).astype(BF16)
    causal = jnp.concatenate([col < row] * heads, axis=0)
    lane = lax.broadcasted_iota(jnp.int32, (blk, width), 1)

    def prep(i, _):
        q = q_ref[pl.ds(pl.multiple_of(i * blk, blk), blk), :]
        for hh in range(heads):
            keep = (lane >= hh * DH_C) & (lane < (hh + 1) * DH_C)
            qm_s[i, hh * blk:(hh + 1) * blk, :] = jnp.where(keep, q, jnp.zeros_like(q))
        c_s[i] = jnp.zeros((rows, width), F32)
        acc_s[i] = jnp.zeros((width, blk), F32)
        return 0

    lax.fori_loop(0, nblk, prep, 0)

    def scores(idx, slot):
        z = _dot(qm_s[idx[0]], kt_ref[idx[1]])
        zn_s[slot] = jnp.minimum(z, Z_MAX)

    def gate_logs(slot, diag):
        z = zn_s[slot]
        lm = jnp.log(1.0 + jnp.exp2(z)) * (-LOG2E)
        lb = lm + z
        if diag:
            lm = jnp.where(causal, lm, 0.0)
        suf = _dot(lm.astype(BF16), later)
        p = lb + suf
        if diag:
            p = jnp.where(causal, p, NEG_BIG)
        p_s[slot] = p
        tot_s[slot] = jnp.broadcast_to(suf[:, 0:1] + lm[:, 0:1], (rows, width))

    def weigh(idx, slot):
        c = c_s[idx[0]]
        w = jnp.exp2(p_s[slot] + jnp.concatenate([c] * (blk // width), axis=1))
        pv = lax.dot_general(vt_ref[idx[1]], w.astype(BF16), (((1,), (1,)), ((), ())),
                             preferred_element_type=F32)
        c_s[idx[0]] = c + tot_s[slot]
        for hh in range(heads):
            acc_s[idx[0], hh * DH_C:(hh + 1) * DH_C, :] += pv[hh * DH_C:(hh + 1) * DH_C,
                                                              hh * blk:(hh + 1) * blk]

    def run_tiles(ntiles, first, step, diag):
        depth = PIPE_DEPTH
        if ntiles < 2 * depth:
            idx = first
            for _ in range(ntiles):
                scores(idx, 0)
                gate_logs(0, diag)
                weigh(idx, 0)
                idx = step(idx)
            return

        def one_step(win, n, do_weigh=True, do_logs=True, do_scores=True):
            slot = n % depth
            nxt = step(win[-1])
            if do_weigh:
                weigh(win[0], slot)
            if do_logs:
                gate_logs(slot, diag)
            if do_scores:
                scores(nxt, slot)
            return win[1:] + (nxt,)

        win = (first,) * (2 * depth)
        scores(first, 0)
        for n in range(1, 2 * depth):
            win = one_step(win, n, do_weigh=False, do_logs=n >= depth)
        steady = ntiles - 2 * depth
        n0 = 2 * depth
        for n in range(n0, n0 + steady % depth):
            win = one_step(win, n)
        n0 += steady % depth

        def body(_, win):
            for k in range(depth):
                win = one_step(win, n0 + k)
            return win

        win = lax.fori_loop(0, steady // depth, body, win)
        for n in range(ntiles, ntiles + 2 * depth):
            win = one_step(win, n, do_logs=n < ntiles + depth, do_scores=False)

    one = jnp.int32(1)
    zero = jnp.int32(0)
    run_tiles(nblk, (zero, zero), lambda ij: (ij[0] + 1, ij[1] + 1), True)
    run_tiles(nblk * (nblk - 1) // 2, (one, zero),
              lambda ij: (jnp.where(ij[1] > 0, ij[0], ij[0] + 1),
                          jnp.where(ij[1] > 0, ij[1] - 1, ij[0])), False)

    def finish(i, _):
        q0 = pl.multiple_of(i * blk, blk)
        o_ref[pl.ds(q0, blk), :] = (acc_s[i].T * sz_ref[pl.ds(q0, blk), :].astype(F32)).astype(BF16)
        return 0

    lax.fori_loop(0, nblk, finish, 0)


def _attention(q, kt, vt, sz):
    bsz, s, d = q.shape
    assert s % ATTN_BLOCK == 0 and d % LANES == 0 and LANES % DH_C == 0
    nblk = s // ATTN_BLOCK
    heads = LANES // DH_C
    spec = lambda: pl.BlockSpec((None, s, LANES), lambda b, h: (b, 0, h))
    tposed = lambda: pl.BlockSpec((None, None, nblk, LANES, ATTN_BLOCK), lambda b, h: (b, h, 0, 0, 0))
    return pl.pallas_call(
        _attn_body,
        grid=(bsz, d // LANES),
        in_specs=[spec(), tposed(), tposed(), spec()],
        out_specs=spec(),
        out_shape=jax.ShapeDtypeStruct((bsz, s, d), BF16),
        scratch_shapes=[
            pltpu.VMEM((nblk, heads * ATTN_BLOCK, LANES), BF16),
            pltpu.VMEM((nblk, heads * ATTN_BLOCK, LANES), F32),
            pltpu.VMEM((nblk, LANES, ATTN_BLOCK), F32),
            pltpu.VMEM((PIPE_DEPTH, heads * ATTN_BLOCK, ATTN_BLOCK), F32),
            pltpu.VMEM((PIPE_DEPTH, heads * ATTN_BLOCK, ATTN_BLOCK), F32),
            pltpu.VMEM((PIPE_DEPTH, heads * ATTN_BLOCK, LANES), F32),
        ],
        compiler_params=pltpu.CompilerParams(
            dimension_semantics=("arbitrary", "arbitrary"),
            vmem_limit_bytes=VMEM_LIMIT_BYTES),
        name="stick_breaking_attention",
    )(q, kt, vt, sz)


def _outproj_body(y_ref, x_ref, gate_ref, w_ref, o_ref):
    o_ref[...] = x_ref[...] + gate_ref[...] * _dot(y_ref[...], w_ref[...])


def _outproj(y, x, gate, w_out):
    bsz, s, d = x.shape
    tm = min(ROW_TILE, s)
    assert s % tm == 0
    return pl.pallas_call(
        _outproj_body,
        grid=(bsz, s // tm),
        in_specs=[
            pl.BlockSpec((None, tm, y.shape[2]), lambda b, i: (b, i, 0)),
            pl.BlockSpec((None, tm, d), lambda b, i: (b, i, 0)),
            pl.BlockSpec((None, 1, d), lambda b, i: (b, 0, 0)),
            _const_spec(w_out.shape),
        ],
        out_specs=pl.BlockSpec((None, tm, d), lambda b, i: (b, i, 0)),
        out_shape=jax.ShapeDtypeStruct((bsz, s, d), F32),
        compiler_params=pltpu.CompilerParams(
            dimension_semantics=("arbitrary", "arbitrary"),
            vmem_limit_bytes=VMEM_LIMIT_BYTES),
        name="layer1_out_proj",
    )(y, x, gate, w_out.astype(BF16))


def kernel(x, c, ln_g, ada_w, ada_b, w_in_ab, conv_w, sg_norm, sg_w, sg_b, w_out_ab,
           w_in_c, q_norm, k_norm, w_out_c):
    bsz, _, d = x.shape
    depth = ada_w.shape[0]
    mod = _modulation(c, ada_w, ada_b)
    for l in range(depth):
        shift, scale, gate = (mod[l, :, j * d:(j + 1) * d].reshape(bsz, 1, d) for j in range(3))
        i = l // 2
        if l % 2 == 0:
            x = _layer0(x, shift, scale, gate, ln_g[l], w_in_ab[i], conv_w[i], sg_norm[i],
                        sg_w[i], sg_b[i], w_out_ab[i])
        else:
            q, kt, vt, sz = _inproj1(x, shift, scale, ln_g[l], w_in_c[i], q_norm[i], k_norm[i])
            y = _attention(q, kt, vt, sz)
            x = _outproj(y, x, gate, w_out_c[i])
    return x
```

```python
import functools

import jax
import jax.numpy as jnp
from jax import lax
from jax.experimental import pallas as pl
from jax.experimental.pallas import tpu as pltpu

F32 = jnp.float32
BF16 = jnp.bfloat16

EPS = 1e-6
CONV_W = 3
DH_B = 128
CHUNK = 128
DH_C = 64
LOG2E = 1.4426950408889634

LANES = 128
SUBLANES = 8
MXU_TILE = 256
VMEM_LIMIT_BYTES = 56 * 1024 * 1024

COL_BLOCK = MXU_TILE
ROW_TILE = 512
ATTN_BLOCK = 256


def _const_spec(shape):
    nd = len(shape)
    return pl.BlockSpec(shape, lambda *_: (0,) * nd, pipeline_mode=pl.Buffered(1))


def _dot(a, b):
    return jnp.dot(a, b, preferred_element_type=F32)


def _silu(x):
    return x * jax.nn.sigmoid(x)


def _mod_body(c_ref, w_ref, b_ref, o_ref):
    c = c_ref[...]
    o_ref[0] = jnp.dot(_silu(c), w_ref[0], preferred_element_type=F32,
                       precision=lax.Precision.HIGHEST) + b_ref[0]


def _modulation(c, ada_w, ada_b):
    depth, d, n3 = ada_w.shape
    bsz = c.shape[0]
    rows = -(-bsz // SUBLANES) * SUBLANES
    c_pad = jnp.pad(c, ((0, rows - bsz), (0, 0)))
    tn = d
    mod = pl.pallas_call(
        _mod_body,
        grid=(depth, n3 // tn),
        in_specs=[
            pl.BlockSpec((rows, d), lambda l, n: (0, 0)),
            pl.BlockSpec((1, d, tn), lambda l, n: (l, 0, n)),
            pl.BlockSpec((1, 1, tn), lambda l, n: (l, 0, n)),
        ],
        out_specs=pl.BlockSpec((1, rows, tn), lambda l, n: (l, 0, n)),
        out_shape=jax.ShapeDtypeStruct((depth, rows, n3), F32),
        compiler_params=pltpu.CompilerParams(
            dimension_semantics=("arbitrary", "arbitrary"),
            vmem_limit_bytes=VMEM_LIMIT_BYTES),
        name="adaln_modulation",
    )(c_pad, ada_w, ada_b.reshape(depth, 1, n3))
    return mod[:, :bsz, :]


def _modulated_norm(x, g, scale, shift):
    ms = jnp.mean(x * x, axis=-1, keepdims=True)
    xn = x * lax.rsqrt(ms + EPS)
    return (xn * g) * (1.0 + scale) + shift


def _layer0_body(x_ref, shift_ref, scale_ref, gate_ref, g_ref, win_ref, convw_ref,
                 sgn_ref, sgw_ref, sgb_ref, wout_ref, o_ref,
                 h_s, u_s, carry_s, y_s):
    tm, d = x_ref.shape
    d_a = convw_ref.shape[1]
    d_b = sgb_ref.shape[1]
    cw = COL_BLOCK

    @pl.when(pl.program_id(1) == 0)
    def _():
        carry_s[...] = jnp.zeros_like(carry_s)

    x = x_ref[...]
    h_s[...] = _modulated_norm(x, g_ref[...], scale_ref[...], shift_ref[...]).astype(BF16)
    hb = h_s[...]

    for cb in range(d_a // cw):
        c0 = cb * cw
        bg = _dot(hb, win_ref[:, c0:c0 + cw])
        cg = _dot(hb, win_ref[:, d_a + c0:d_a + c0 + cw])
        xa = _dot(hb, win_ref[:, 2 * d_a + c0:2 * d_a + c0 + cw])
        za = _dot(hb, win_ref[:, 3 * d_a + c0:3 * d_a + c0 + cw])
        u = cg * xa
        u_s[0:SUBLANES, :] = carry_s[:, c0:c0 + cw]
        u_s[SUBLANES:SUBLANES + tm, :] = u
        carry_s[:, c0:c0 + cw] = u[tm - SUBLANES:tm, :]
        u1 = u_s[SUBLANES - 1:SUBLANES - 1 + tm, :]
        u2 = u_s[SUBLANES - 2:SUBLANES - 2 + tm, :]
        conv = (convw_ref[2:3, c0:c0 + cw] * u + convw_ref[1:2, c0:c0 + cw] * u1
                + convw_ref[0:1, c0:c0 + cw] * u2)
        y_s[:, c0:c0 + cw] = (bg * conv * _silu(za)).astype(BF16)

    row = lax.broadcasted_iota(jnp.int32, (CHUNK, CHUNK), 0)
    col = lax.broadcasted_iota(jnp.int32, (CHUNK, CHUNK), 1)
    tril = row >= col
    off = 4 * d_a
    for cb in range(d_b // cw):
        c0 = cb * cw
        ub = _dot(hb, win_ref[:, off + c0:off + c0 + cw])
        vb = _dot(hb, win_ref[:, off + d_b + c0:off + d_b + c0 + cw])
        zb = _dot(hb, win_ref[:, off + 2 * d_b + c0:off + 2 * d_b + c0 + cw])
        gz = ub * _silu(zb)
        for gg in range(cw // DH_B):
            grp = (c0 + gg * DH_B) // DH_B
            l0 = gg * DH_B
            vg = vb[:, l0:l0 + DH_B]
            ms = jnp.mean(vg * vg, axis=-1, keepdims=True)
            vn = (vg * lax.rsqrt(ms + EPS) * sgn_ref[grp:grp + 1, :]).astype(BF16)
            wt = jnp.where(tril, sgw_ref[grp], 0.0).astype(BF16)
            bias = sgb_ref[:, c0 + l0:c0 + l0 + DH_B]
            for ch in range(tm // CHUNK):
                r0 = ch * CHUNK
                sgate = _dot(wt, vn[r0:r0 + CHUNK, :]) + bias
                y_s[r0:r0 + CHUNK, d_a + c0 + l0:d_a + c0 + l0 + DH_B] = (
                    gz[r0:r0 + CHUNK, l0:l0 + DH_B] * sgate).astype(BF16)

    out = _dot(y_s[...], wout_ref[...])
    o_ref[...] = x + gate_ref[...] * out


def _layer0(x, shift, scale, gate, ln_g, w_in, conv_w, sg_norm, sg_w, sg_b, w_out):
    bsz, s, d = x.shape
    d_a = conv_w.shape[1]
    g_b = sg_norm.shape[0]
    d_b = g_b * DH_B
    tm = min(ROW_TILE, s)
    assert s % tm == 0 and tm % CHUNK == 0 and d_a % COL_BLOCK == 0 and d_b % COL_BLOCK == 0
    bias_full = jnp.broadcast_to(sg_b.T[:, :, None], (CHUNK, g_b, DH_B)).reshape(CHUNK, d_b)
    vec = lambda: pl.BlockSpec((None, 1, d), lambda b, i: (b, 0, 0))
    return pl.pallas_call(
        _layer0_body,
        grid=(bsz, s // tm),
        in_specs=[
            pl.BlockSpec((None, tm, d), lambda b, i: (b, i, 0)),
            vec(), vec(), vec(),
            _const_spec((1, d)),
            _const_spec(w_in.shape),
            _const_spec(conv_w.shape),
            _const_spec(sg_norm.shape),
            _const_spec(sg_w.shape),
            _const_spec(bias_full.shape),
            _const_spec(w_out.shape),
        ],
        out_specs=pl.BlockSpec((None, tm, d), lambda b, i: (b, i, 0)),
        out_shape=jax.ShapeDtypeStruct((bsz, s, d), F32),
        scratch_shapes=[
            pltpu.VMEM((tm, d), BF16),
            pltpu.VMEM((tm + SUBLANES, COL_BLOCK), F32),
            pltpu.VMEM((SUBLANES, d_a), F32),
            pltpu.VMEM((tm, d_a + d_b), BF16),
        ],
        compiler_params=pltpu.CompilerParams(
            dimension_semantics=("arbitrary", "arbitrary"),
            vmem_limit_bytes=VMEM_LIMIT_BYTES),
        name="layer0_conv_sgmlp",
    )(x, shift, scale, gate, ln_g.reshape(1, d), w_in.astype(BF16), conv_w, sg_norm, sg_w,
      bias_full, w_out.astype(BF16))


def _inproj1_body(x_ref, shift_ref, scale_ref, g_ref, w_ref, qg_ref, kg_ref,
                  q_o, k_o, vt_o, sz_o, h_s, raw_s):
    tm, d = x_ref.shape
    cw = COL_BLOCK
    h_s[...] = _modulated_norm(x_ref[...], g_ref[...], scale_ref[...], shift_ref[...]).astype(BF16)
    hb = h_s[...]
    for cb in range(2 * d // cw):
        raw_s[:, cb * cw:(cb + 1) * cw] = _dot(hb, w_ref[:, cb * cw:(cb + 1) * cw])
    for cb in range(d // cw):
        c0 = cb * cw
        v = _dot(hb, w_ref[:, 2 * d + c0:2 * d + c0 + cw])
        for kb in range(tm // ATTN_BLOCK):
            for hp in range(cw // LANES):
                vt_o[c0 // LANES + hp, kb] = v[kb * ATTN_BLOCK:(kb + 1) * ATTN_BLOCK,
                                              hp * LANES:(hp + 1) * LANES].T.astype(BF16)
        sz_o[:, c0:c0 + cw] = _silu(_dot(hb, w_ref[:, 3 * d + c0:3 * d + c0 + cw])).astype(BF16)

    row = lax.broadcasted_iota(jnp.int32, (cw, cw), 0)
    col = lax.broadcasted_iota(jnp.int32, (cw, cw), 1)
    bd = jnp.where((row // DH_C) == (col // DH_C), 1.0, 0.0).astype(BF16)
    q_scale = LOG2E * (DH_C ** -0.5)
    for cb in range(2 * d // cw):
        c0 = cb * cw
        t = raw_s[:, c0:c0 + cw]
        ss = _dot((t * t).astype(BF16), bd)
        tn = t * lax.rsqrt(ss * (1.0 / DH_C) + EPS)
        if c0 < d:
            q_o[:, c0:c0 + cw] = (tn * qg_ref[:, c0:c0 + cw] * q_scale).astype(BF16)
        else:
            k_o[:, c0 - d:c0 - d + cw] = (tn * kg_ref[:, c0 - d:c0 - d + cw]).astype(BF16)


def _inproj1(x, shift, scale, ln_g, w_in, q_norm, k_norm):
    bsz, s, d = x.shape
    tm = min(ROW_TILE, s)
    assert s % tm == 0 and d % COL_BLOCK == 0 and COL_BLOCK % DH_C == 0
    assert tm % ATTN_BLOCK == 0 and COL_BLOCK % LANES == 0
    heads = d // DH_C
    vec = lambda: pl.BlockSpec((None, 1, d), lambda b, i: (b, 0, 0))
    tile = lambda: pl.BlockSpec((None, tm, d), lambda b, i: (b, i, 0))
    out = jax.ShapeDtypeStruct((bsz, s, d), BF16)
    return pl.pallas_call(
        _inproj1_body,
        grid=(bsz, s // tm),
        in_specs=[tile(), vec(), vec(), _const_spec((1, d)), _const_spec(w_in.shape),
                  _const_spec((1, d)), _const_spec((1, d))],
        out_specs=[tile(), tile(),
                   pl.BlockSpec((None, d // LANES, tm // ATTN_BLOCK, LANES, ATTN_BLOCK),
                                lambda b, i: (b, 0, i, 0, 0)),
                   tile()],
        out_shape=[out, out,
                   jax.ShapeDtypeStruct((bsz, d // LANES, s // ATTN_BLOCK, LANES, ATTN_BLOCK), BF16),
                   out],
        scratch_shapes=[pltpu.VMEM((tm, d), BF16), pltpu.VMEM((tm, 2 * d), F32)],
        compiler_params=pltpu.CompilerParams(
            dimension_semantics=("arbitrary", "arbitrary"),
            vmem_limit_bytes=VMEM_LIMIT_BYTES),
        name="layer1_qkvz_proj",
    )(x, shift, scale, ln_g.reshape(1, d), w_in.astype(BF16),
      jnp.tile(q_norm, heads).reshape(1, d), jnp.tile(k_norm, heads).reshape(1, d))


PIPE_DEPTH = 8
Z_MAX = 126.0
FINISH_UNROLL = 8
NEG_BIG = -1e30


def _attn_body(q_ref, k_ref, vt_ref, sz_ref, o_ref, qm_s, c_s, acc_s, zn_s, p_s, tot_s):
    s, width = q_ref.shape
    blk = ATTN_BLOCK
    nblk = s // blk
    heads = width // DH_C
    rows = heads * blk

    row = lax.broadcasted_iota(jnp.int32, (blk, blk), 0)
    col = lax.broadcasted_iota(jnp.int32, (blk, blk), 1)
    later = jnp.where(row > col, 1.0, 0.0).astype(BF16)
    causal = jnp.concatenate([col < row] * heads, axis=0)
    lane = lax.broadcasted_iota(jnp.int32, (blk, width), 1)

    def prep(i, _):
        q = q_ref[pl.ds(pl.multiple_of(i * blk, blk), blk), :]
        for hh in range(heads):
            keep = (lane >= hh * DH_C) & (lane < (hh + 1) * DH_C)
            qm_s[i, hh * blk:(hh + 1) * blk, :] = jnp.where(keep, q, jnp.zeros_like(q))
        c_s[i] = jnp.zeros((rows, width), F32)
        acc_s[i] = jnp.zeros((width, blk), F32)
        return 0

    lax.fori_loop(0, nblk, prep, 0)

    def scores(idx, slot):
        kb = k_ref[pl.ds(pl.multiple_of(idx[1] * blk, blk), blk), :]
        z = lax.dot_general(qm_s[idx[0]], kb, (((1,), (1,)), ((), ())),
                            preferred_element_type=F32)
        zn_s[slot] = jnp.minimum(z, Z_MAX)

    def gate_logs(slot, diag):
        z = zn_s[slot]
        lm = jnp.log(1.0 + jnp.exp2(z)) * (-LOG2E)
        lb = lm + z
        if diag:
            lm = jnp.where(causal, lm, 0.0)
        suf = _dot(lm.astype(BF16), later)
        p = lb + suf
        if diag:
            p = jnp.where(causal, p, NEG_BIG)
        p_s[slot] = p
        tot_s[slot] = jnp.broadcast_to(suf[:, 0:1] + lm[:, 0:1], (rows, width))

    def weigh(idx, slot):
        c = c_s[idx[0]]
        w = jnp.exp2(p_s[slot] + jnp.concatenate([c] * (blk // width), axis=1))
        pv = lax.dot_general(vt_ref[idx[1]], w.astype(BF16), (((1,), (1,)), ((), ())),
                             preferred_element_type=F32)
        c_s[idx[0]] = c + tot_s[slot]
        for hh in range(heads):
            acc_s[idx[0], hh * DH_C:(hh + 1) * DH_C, :] += pv[hh * DH_C:(hh + 1) * DH_C,
                                                              hh * blk:(hh + 1) * blk]

    def run_tiles(ntiles, first, step, diag):
        depth = PIPE_DEPTH
        if ntiles < 2 * depth:
            idx = first
            for _ in range(ntiles):
                scores(idx, 0)
                gate_logs(0, diag)
                weigh(idx, 0)
                idx = step(idx)
            return

        def one_step(win, n, do_weigh=True, do_logs=True, do_scores=True):
            slot = n % depth
            nxt = step(win[-1])
            if do_weigh:
                weigh(win[0], slot)
            if do_logs:
                gate_logs(slot, diag)
            if do_scores:
                scores(nxt, slot)
            return win[1:] + (nxt,)

        win = (first,) * (2 * depth)
        scores(first, 0)
        for n in range(1, 2 * depth):
            win = one_step(win, n, do_weigh=False, do_logs=n >= depth)
        steady = ntiles - 2 * depth
        n0 = 2 * depth
        for n in range(n0, n0 + steady % depth):
            win = one_step(win, n)
        n0 += steady % depth

        def body(_, win):
            for k in range(depth):
                win = one_step(win, n0 + k)
            return win

        win = lax.fori_loop(0, steady // depth, body, win)
        for n in range(ntiles, ntiles + 2 * depth):
            win = one_step(win, n, do_logs=n < ntiles + depth, do_scores=False)

    one = jnp.int32(1)
    zero = jnp.int32(0)
    run_tiles(nblk, (zero, zero), lambda ij: (ij[0] + 1, ij[1] + 1), True)
    run_tiles(nblk * (nblk - 1) // 2, (one, zero),
              lambda ij: (jnp.where(ij[1] > 0, ij[0], ij[0] + 1),
                          jnp.where(ij[1] > 0, ij[1] - 1, ij[0])), False)

    def finish(i, _):
        q0 = pl.multiple_of(i * blk, blk)
        o_ref[pl.ds(q0, blk), :] = (acc_s[i].T * sz_ref[pl.ds(q0, blk), :].astype(F32)).astype(BF16)
        return 0

    lax.fori_loop(0, nblk, finish, 0, unroll=FINISH_UNROLL if nblk % FINISH_UNROLL == 0 else 1)


def _attention(q, k, vt, sz):
    bsz, s, d = q.shape
    assert s % ATTN_BLOCK == 0 and d % LANES == 0 and LANES % DH_C == 0
    nblk = s // ATTN_BLOCK
    heads = LANES // DH_C
    spec = lambda: pl.BlockSpec((None, s, LANES), lambda b, h: (b, 0, h))
    return pl.pallas_call(
        _attn_body,
        grid=(bsz, d // LANES),
        in_specs=[spec(), spec(),
                  pl.BlockSpec((None, None, nblk, LANES, ATTN_BLOCK), lambda b, h: (b, h, 0, 0, 0)),
                  spec()],
        out_specs=spec(),
        out_shape=jax.ShapeDtypeStruct((bsz, s, d), BF16),
        scratch_shapes=[
            pltpu.VMEM((nblk, heads * ATTN_BLOCK, LANES), BF16),
            pltpu.VMEM((nblk, heads * ATTN_BLOCK, LANES), F32),
            pltpu.VMEM((nblk, LANES, ATTN_BLOCK), F32),
            pltpu.VMEM((PIPE_DEPTH, heads * ATTN_BLOCK, ATTN_BLOCK), F32),
            pltpu.VMEM((PIPE_DEPTH, heads * ATTN_BLOCK, ATTN_BLOCK), F32),
            pltpu.VMEM((PIPE_DEPTH, heads * ATTN_BLOCK, LANES), F32),
        ],
        compiler_params=pltpu.CompilerParams(
            dimension_semantics=("arbitrary", "arbitrary"),
            vmem_limit_bytes=VMEM_LIMIT_BYTES),
        name="stick_breaking_attention",
    )(q, k, vt, sz)


def _outproj_body(y_ref, x_ref, gate_ref, w_ref, o_ref):
    o_ref[...] = x_ref[...] + gate_ref[...] * _dot(y_ref[...], w_ref[...])


def _outproj(y, x, gate, w_out):
    bsz, s, d = x.shape
    tm = min(ROW_TILE, s)
    assert s % tm == 0
    return pl.pallas_call(
        _outproj_body,
        grid=(bsz, s // tm),
        in_specs=[
            pl.BlockSpec((None, tm, y.shape[2]), lambda b, i: (b, i, 0)),
            pl.BlockSpec((None, tm, d), lambda b, i: (b, i, 0)),
            pl.BlockSpec((None, 1, d), lambda b, i: (b, 0, 0)),
            _const_spec(w_out.shape),
        ],
        out_specs=pl.BlockSpec((None, tm, d), lambda b, i: (b, i, 0)),
        out_shape=jax.ShapeDtypeStruct((bsz, s, d), F32),
        compiler_params=pltpu.CompilerParams(
            dimension_semantics=("arbitrary", "arbitrary"),
            vmem_limit_bytes=VMEM_LIMIT_BYTES),
        name="layer1_out_proj",
    )(y, x, gate, w_out.astype(BF16))


def kernel(x, c, ln_g, ada_w, ada_b, w_in_ab, conv_w, sg_norm, sg_w, sg_b, w_out_ab,
           w_in_c, q_norm, k_norm, w_out_c):
    bsz, _, d = x.shape
    depth = ada_w.shape[0]
    mod = _modulation(c, ada_w, ada_b)
    for l in range(depth):
        shift, scale, gate = (mod[l, :, j * d:(j + 1) * d].reshape(bsz, 1, d) for j in range(3))
        i = l // 2
        if l % 2 == 0:
            x = _layer0(x, shift, scale, gate, ln_g[l], w_in_ab[i], conv_w[i], sg_norm[i],
                        sg_w[i], sg_b[i], w_out_ab[i])
        else:
            q, k, vt, sz = _inproj1(x, shift, scale, ln_g[l], w_in_c[i], q_norm[i], k_norm[i])
            y = _attention(q, k, vt, sz)
            x = _outproj(y, x, gate, w_out_c[i])
    return x
```

```python
import functools

import jax
import jax.numpy as jnp
from jax import lax
from jax.experimental import pallas as pl
from jax.experimental.pallas import tpu as pltpu

F32 = jnp.float32
BF16 = jnp.bfloat16

EPS = 1e-6
CONV_W = 3
DH_B = 128
CHUNK = 128
DH_C = 64
LOG2E = 1.4426950408889634

LANES = 128
SUBLANES = 8
MXU_TILE = 256
VMEM_LIMIT_BYTES = 56 * 1024 * 1024

COL_BLOCK = MXU_TILE
ROW_TILE = 512
ATTN_BLOCK = 256


def _const_spec(shape):
    nd = len(shape)
    return pl.BlockSpec(shape, lambda *_: (0,) * nd, pipeline_mode=pl.Buffered(1))


def _dot(a, b):
    return jnp.dot(a, b, preferred_element_type=F32)


def _silu(x):
    return x * jax.nn.sigmoid(x)


def _mod_body(c_ref, w_ref, b_ref, o_ref):
    c = c_ref[...]
    o_ref[0] = jnp.dot(_silu(c), w_ref[0], preferred_element_type=F32,
                       precision=lax.Precision.HIGHEST) + b_ref[0]


def _modulation(c, ada_w, ada_b):
    depth, d, n3 = ada_w.shape
    bsz = c.shape[0]
    rows = -(-bsz // SUBLANES) * SUBLANES
    c_pad = jnp.pad(c, ((0, rows - bsz), (0, 0)))
    tn = d
    mod = pl.pallas_call(
        _mod_body,
        grid=(depth, n3 // tn),
        in_specs=[
            pl.BlockSpec((rows, d), lambda l, n: (0, 0)),
            pl.BlockSpec((1, d, tn), lambda l, n: (l, 0, n)),
            pl.BlockSpec((1, 1, tn), lambda l, n: (l, 0, n)),
        ],
        out_specs=pl.BlockSpec((1, rows, tn), lambda l, n: (l, 0, n)),
        out_shape=jax.ShapeDtypeStruct((depth, rows, n3), F32),
        compiler_params=pltpu.CompilerParams(
            dimension_semantics=("arbitrary", "arbitrary"),
            vmem_limit_bytes=VMEM_LIMIT_BYTES),
        name="adaln_modulation",
    )(c_pad, ada_w, ada_b.reshape(depth, 1, n3))
    return mod[:, :bsz, :]


def _modulated_norm(x, g, scale, shift):
    ms = jnp.mean(x * x, axis=-1, keepdims=True)
    xn = x * lax.rsqrt(ms + EPS)
    return (xn * g) * (1.0 + scale) + shift


def _layer0_body(x_ref, shift_ref, scale_ref, gate_ref, g_ref, win_ref, convw_ref,
                 sgn_ref, sgw_ref, sgb_ref, wout_ref, o_ref,
                 h_s, u_s, carry_s, y_s):
    tm, d = x_ref.shape
    d_a = convw_ref.shape[1]
    d_b = sgb_ref.shape[1]
    cw = COL_BLOCK

    @pl.when(pl.program_id(1) == 0)
    def _():
        carry_s[...] = jnp.zeros_like(carry_s)

    x = x_ref[...]
    h_s[...] = _modulated_norm(x, g_ref[...], scale_ref[...], shift_ref[...]).astype(BF16)
    hb = h_s[...]

    for cb in range(d_a // cw):
        c0 = cb * cw
        bg = _dot(hb, win_ref[:, c0:c0 + cw])
        cg = _dot(hb, win_ref[:, d_a + c0:d_a + c0 + cw])
        xa = _dot(hb, win_ref[:, 2 * d_a + c0:2 * d_a + c0 + cw])
        za = _dot(hb, win_ref[:, 3 * d_a + c0:3 * d_a + c0 + cw])
        u = cg * xa
        u_s[0:SUBLANES, :] = carry_s[:, c0:c0 + cw]
        u_s[SUBLANES:SUBLANES + tm, :] = u
        carry_s[:, c0:c0 + cw] = u[tm - SUBLANES:tm, :]
        u1 = u_s[SUBLANES - 1:SUBLANES - 1 + tm, :]
        u2 = u_s[SUBLANES - 2:SUBLANES - 2 + tm, :]
        conv = (convw_ref[2:3, c0:c0 + cw] * u + convw_ref[1:2, c0:c0 + cw] * u1
                + convw_ref[0:1, c0:c0 + cw] * u2)
        y_s[:, c0:c0 + cw] = (bg * conv * _silu(za)).astype(BF16)

    row = lax.broadcasted_iota(jnp.int32, (CHUNK, CHUNK), 0)
    col = lax.broadcasted_iota(jnp.int32, (CHUNK, CHUNK), 1)
    tril = row >= col
    off = 4 * d_a
    for cb in range(d_b // cw):
        c0 = cb * cw
        ub = _dot(hb, win_ref[:, off + c0:off + c0 + cw])
        vb = _dot(hb, win_ref[:, off + d_b + c0:off + d_b + c0 + cw])
        zb = _dot(hb, win_ref[:, off + 2 * d_b + c0:off + 2 * d_b + c0 + cw])
        gz = ub * _silu(zb)
        for gg in range(cw // DH_B):
            grp = (c0 + gg * DH_B) // DH_B
            l0 = gg * DH_B
            vg = vb[:, l0:l0 + DH_B]
            ms = jnp.mean(vg * vg, axis=-1, keepdims=True)
            vn = (vg * lax.rsqrt(ms + EPS) * sgn_ref[grp:grp + 1, :]).astype(BF16)
            wt = jnp.where(tril, sgw_ref[grp], 0.0).astype(BF16)
            bias = sgb_ref[:, c0 + l0:c0 + l0 + DH_B]
            for ch in range(tm // CHUNK):
                r0 = ch * CHUNK
                sgate = _dot(wt, vn[r0:r0 + CHUNK, :]) + bias
                y_s[r0:r0 + CHUNK, d_a + c0 + l0:d_a + c0 + l0 + DH_B] = (
                    gz[r0:r0 + CHUNK, l0:l0 + DH_B] * sgate).astype(BF16)

    out = _dot(y_s[...], wout_ref[...])
    o_ref[...] = x + gate_ref[...] * out


def _layer0(x, shift, scale, gate, ln_g, w_in, conv_w, sg_norm, sg_w, sg_b, w_out):
    bsz, s, d = x.shape
    d_a = conv_w.shape[1]
    g_b = sg_norm.shape[0]
    d_b = g_b * DH_B
    tm = min(ROW_TILE, s)
    assert s % tm == 0 and tm % CHUNK == 0 and d_a % COL_BLOCK == 0 and d_b % COL_BLOCK == 0
    bias_full = jnp.broadcast_to(sg_b.T[:, :, None], (CHUNK, g_b, DH_B)).reshape(CHUNK, d_b)
    vec = lambda: pl.BlockSpec((None, 1, d), lambda b, i: (b, 0, 0))
    return pl.pallas_call(
        _layer0_body,
        grid=(bsz, s // tm),
        in_specs=[
            pl.BlockSpec((None, tm, d), lambda b, i: (b, i, 0)),
            vec(), vec(), vec(),
            _const_spec((1, d)),
            _const_spec(w_in.shape),
            _const_spec(conv_w.shape),
            _const_spec(sg_norm.shape),
            _const_spec(sg_w.shape),
            _const_spec(bias_full.shape),
            _const_spec(w_out.shape),
        ],
        out_specs=pl.BlockSpec((None, tm, d), lambda b, i: (b, i, 0)),
        out_shape=jax.ShapeDtypeStruct((bsz, s, d), F32),
        scratch_shapes=[
            pltpu.VMEM((tm, d), BF16),
            pltpu.VMEM((tm + SUBLANES, COL_BLOCK), F32),
            pltpu.VMEM((SUBLANES, d_a), F32),
            pltpu.VMEM((tm, d_a + d_b), BF16),
        ],
        compiler_params=pltpu.CompilerParams(
            dimension_semantics=("arbitrary", "arbitrary"),
            vmem_limit_bytes=VMEM_LIMIT_BYTES),
        name="layer0_conv_sgmlp",
    )(x, shift, scale, gate, ln_g.reshape(1, d), w_in.astype(BF16), conv_w, sg_norm, sg_w,
      bias_full, w_out.astype(BF16))


def _inproj1_body(x_ref, shift_ref, scale_ref, g_ref, w_ref, qg_ref, kg_ref,
                  q_o, k_o, vt_o, sz_o, h_s, raw_s):
    tm, d = x_ref.shape
    cw = COL_BLOCK
    h_s[...] = _modulated_norm(x_ref[...], g_ref[...], scale_ref[...], shift_ref[...]).astype(BF16)
    hb = h_s[...]
    for cb in range(2 * d // cw):
        raw_s[:, cb * cw:(cb + 1) * cw] = _dot(hb, w_ref[:, cb * cw:(cb + 1) * cw])
    for cb in range(d // cw):
        c0 = cb * cw
        v = _dot(hb, w_ref[:, 2 * d + c0:2 * d + c0 + cw])
        for kb in range(tm // ATTN_BLOCK):
            for hp in range(cw // LANES):
                vt_o[c0 // LANES + hp, kb] = v[kb * ATTN_BLOCK:(kb + 1) * ATTN_BLOCK,
                                              hp * LANES:(hp + 1) * LANES].T.astype(BF16)
        sz_o[:, c0:c0 + cw] = _silu(_dot(hb, w_ref[:, 3 * d + c0:3 * d + c0 + cw])).astype(BF16)

    row = lax.broadcasted_iota(jnp.int32, (cw, cw), 0)
    col = lax.broadcasted_iota(jnp.int32, (cw, cw), 1)
    bd = jnp.where((row // DH_C) == (col // DH_C), 1.0, 0.0).astype(BF16)
    q_scale = LOG2E * (DH_C ** -0.5)
    for cb in range(2 * d // cw):
        c0 = cb * cw
        t = raw_s[:, c0:c0 + cw]
        ss = _dot((t * t).astype(BF16), bd)
        tn = t * lax.rsqrt(ss * (1.0 / DH_C) + EPS)
        if c0 < d:
            q_o[:, c0:c0 + cw] = (tn * qg_ref[:, c0:c0 + cw] * q_scale).astype(BF16)
        else:
            k_o[:, c0 - d:c0 - d + cw] = (tn * kg_ref[:, c0 - d:c0 - d + cw]).astype(BF16)


def _inproj1(x, shift, scale, ln_g, w_in, q_norm, k_norm):
    bsz, s, d = x.shape
    tm = min(ROW_TILE, s)
    assert s % tm == 0 and d % COL_BLOCK == 0 and COL_BLOCK % DH_C == 0
    assert tm % ATTN_BLOCK == 0 and COL_BLOCK % LANES == 0
    heads = d // DH_C
    vec = lambda: pl.BlockSpec((None, 1, d), lambda b, i: (b, 0, 0))
    tile = lambda: pl.BlockSpec((None, tm, d), lambda b, i: (b, i, 0))
    out = jax.ShapeDtypeStruct((bsz, s, d), BF16)
    return pl.pallas_call(
        _inproj1_body,
        grid=(bsz, s // tm),
        in_specs=[tile(), vec(), vec(), _const_spec((1, d)), _const_spec(w_in.shape),
                  _const_spec((1, d)), _const_spec((1, d))],
        out_specs=[tile(), tile(),
                   pl.BlockSpec((None, d // LANES, tm // ATTN_BLOCK, LANES, ATTN_BLOCK),
                                lambda b, i: (b, 0, i, 0, 0)),
                   tile()],
        out_shape=[out, out,
                   jax.ShapeDtypeStruct((bsz, d // LANES, s // ATTN_BLOCK, LANES, ATTN_BLOCK), BF16),
                   out],
        scratch_shapes=[pltpu.VMEM((tm, d), BF16), pltpu.VMEM((tm, 2 * d), F32)],
        compiler_params=pltpu.CompilerParams(
            dimension_semantics=("arbitrary", "arbitrary"),
            vmem_limit_bytes=VMEM_LIMIT_BYTES),
        name="layer1_qkvz_proj",
    )(x, shift, scale, ln_g.reshape(1, d), w_in.astype(BF16),
      jnp.tile(q_norm, heads).reshape(1, d), jnp.tile(k_norm, heads).reshape(1, d))


PIPE_DEPTH = 8
Z_MAX = 126.0
FINISH_UNROLL = 8
NEG_BIG = -1e30


def _attn_body(q_ref, k_ref, vt_ref, sz_ref, o_ref, qm_s, c_s, acc_s, zn_s, p_s, tot_s):
    s, width = q_ref.shape
    blk = ATTN_BLOCK
    nblk = s // blk
    heads = width // DH_C
    rows = heads * blk

    row = lax.broadcasted_iota(jnp.int32, (blk, blk), 0)
    col = lax.broadcasted_iota(jnp.int32, (blk, blk), 1)
    later = jnp.where(row > col, 1.0, 0.0).astype(BF16)
    causal = jnp.concatenate([col < row] * heads, axis=0)
    lane = lax.broadcasted_iota(jnp.int32, (blk, width), 1)

    def prep(i, _):
        q = q_ref[pl.ds(pl.multiple_of(i * blk, blk), blk), :]
        for hh in range(heads):
            keep = (lane >= hh * DH_C) & (lane < (hh + 1) * DH_C)
            qm_s[i, hh * blk:(hh + 1) * blk, :] = jnp.where(keep, q, jnp.zeros_like(q))
        return 0

    lax.fori_loop(0, nblk, prep, 0)

    def scores(idx, slot):
        kb = k_ref[pl.ds(pl.multiple_of(idx[1] * blk, blk), blk), :]
        z = lax.dot_general(qm_s[idx[0]], kb, (((1,), (1,)), ((), ())),
                            preferred_element_type=F32)
        zn_s[slot] = jnp.minimum(z, Z_MAX)

    def gate_logs(slot, diag):
        z = zn_s[slot]
        lm = jnp.log(1.0 + jnp.exp2(z)) * (-LOG2E)
        lb = lm + z
        if diag:
            lm = jnp.where(causal, lm, 0.0)
        suf = _dot(lm.astype(BF16), later)
        p = lb + suf
        if diag:
            p = jnp.where(causal, p, NEG_BIG)
        p_s[slot] = p
        tot_s[slot] = jnp.broadcast_to(suf[:, 0:1] + lm[:, 0:1], (rows, width))

    def weigh(idx, slot, first):
        if first:
            w = jnp.exp2(p_s[slot])
            c_s[idx[0]] = tot_s[slot]
        else:
            c = c_s[idx[0]]
            w = jnp.exp2(p_s[slot] + jnp.concatenate([c] * (blk // width), axis=1))
            c_s[idx[0]] = c + tot_s[slot]
        pv = lax.dot_general(vt_ref[idx[1]], w.astype(BF16), (((1,), (1,)), ((), ())),
                             preferred_element_type=F32)
        for hh in range(heads):
            part = pv[hh * DH_C:(hh + 1) * DH_C, hh * blk:(hh + 1) * blk]
            if first:
                acc_s[idx[0], hh * DH_C:(hh + 1) * DH_C, :] = part
            else:
                acc_s[idx[0], hh * DH_C:(hh + 1) * DH_C, :] += part

    def run_tiles(ntiles, first, step, diag):
        depth = PIPE_DEPTH
        if ntiles < 2 * depth:
            idx = first
            for _ in range(ntiles):
                scores(idx, 0)
                gate_logs(0, diag)
                weigh(idx, 0, diag)
                idx = step(idx)
            return

        def one_step(win, n, do_weigh=True, do_logs=True, do_scores=True):
            slot = n % depth
            nxt = step(win[-1])
            if do_weigh:
                weigh(win[0], slot, diag)
            if do_logs:
                gate_logs(slot, diag)
            if do_scores:
                scores(nxt, slot)
            return win[1:] + (nxt,)

        win = (first,) * (2 * depth)
        scores(first, 0)
        for n in range(1, 2 * depth):
            win = one_step(win, n, do_weigh=False, do_logs=n >= depth)
        steady = ntiles - 2 * depth
        n0 = 2 * depth
        for n in range(n0, n0 + steady % depth):
            win = one_step(win, n)
        n0 += steady % depth

        def body(_, win):
            for k in range(depth):
                win = one_step(win, n0 + k)
            return win

        win = lax.fori_loop(0, steady // depth, body, win)
        for n in range(ntiles, ntiles + 2 * depth):
            win = one_step(win, n, do_logs=n < ntiles + depth, do_scores=False)

    one = jnp.int32(1)
    zero = jnp.int32(0)
    run_tiles(nblk, (zero, zero), lambda ij: (ij[0] + 1, ij[1] + 1), True)
    run_tiles(nblk * (nblk - 1) // 2, (one, zero),
              lambda ij: (jnp.where(ij[1] > 0, ij[0], ij[0] + 1),
                          jnp.where(ij[1] > 0, ij[1] - 1, ij[0])), False)

    def finish(i, _):
        q0 = pl.multiple_of(i * blk, blk)
        o_ref[pl.ds(q0, blk), :] = (acc_s[i].T * sz_ref[pl.ds(q0, blk), :].astype(F32)).astype(BF16)
        return 0

    lax.fori_loop(0, nblk, finish, 0, unroll=FINISH_UNROLL if nblk % FINISH_UNROLL == 0 else 1)


def _attention(q, k, vt, sz):
    bsz, s, d = q.shape
    assert s % ATTN_BLOCK == 0 and d % LANES == 0 and LANES % DH_C == 0
    nblk = s // ATTN_BLOCK
    heads = LANES // DH_C
    spec = lambda: pl.BlockSpec((None, s, LANES), lambda b, h: (b, 0, h))
    return pl.pallas_call(
        _attn_body,
        grid=(bsz, d // LANES),
        in_specs=[spec(), spec(),
                  pl.BlockSpec((None, None, nblk, LANES, ATTN_BLOCK), lambda b, h: (b, h, 0, 0, 0)),
                  spec()],
        out_specs=spec(),
        out_shape=jax.ShapeDtypeStruct((bsz, s, d), BF16),
        scratch_shapes=[
            pltpu.VMEM((nblk, heads * ATTN_BLOCK, LANES), BF16),
            pltpu.VMEM((nblk, heads * ATTN_BLOCK, LANES), F32),
            pltpu.VMEM((nblk, LANES, ATTN_BLOCK), F32),
            pltpu.VMEM((PIPE_DEPTH, heads * ATTN_BLOCK, ATTN_BLOCK), F32),
            pltpu.VMEM((PIPE_DEPTH, heads * ATTN_BLOCK, ATTN_BLOCK), F32),
            pltpu.VMEM((PIPE_DEPTH, heads * ATTN_BLOCK, LANES), F32),
        ],
        compiler_params=pltpu.CompilerParams(
            dimension_semantics=("arbitrary", "arbitrary"),
            vmem_limit_bytes=VMEM_LIMIT_BYTES),
        name="stick_breaking_attention",
    )(q, k, vt, sz)


def _outproj_body(y_ref, x_ref, gate_ref, w_ref, o_ref):
    o_ref[...] = x_ref[...] + gate_ref[...] * _dot(y_ref[...], w_ref[...])


def _outproj(y, x, gate, w_out):
    bsz, s, d = x.shape
    tm = min(ROW_TILE, s)
    assert s % tm == 0
    return pl.pallas_call(
        _outproj_body,
        grid=(bsz, s // tm),
        in_specs=[
            pl.BlockSpec((None, tm, y.shape[2]), lambda b, i: (b, i, 0)),
            pl.BlockSpec((None, tm, d), lambda b, i: (b, i, 0)),
            pl.BlockSpec((None, 1, d), lambda b, i: (b, 0, 0)),
            _const_spec(w_out.shape),
        ],
        out_specs=pl.BlockSpec((None, tm, d), lambda b, i: (b, i, 0)),
        out_shape=jax.ShapeDtypeStruct((bsz, s, d), F32),
        compiler_params=pltpu.CompilerParams(
            dimension_semantics=("arbitrary", "arbitrary"),
            vmem_limit_bytes=VMEM_LIMIT_BYTES),
        name="layer1_out_proj",
    )(y, x, gate, w_out.astype(BF16))


def kernel(x, c, ln_g, ada_w, ada_b, w_in_ab, conv_w, sg_norm, sg_w, sg_b, w_out_ab,
           w_in_c, q_norm, k_norm, w_out_c):
    bsz, _, d = x.shape
    depth = ada_w.shape[0]
    mod = _modulation(c, ada_w, ada_b)
    for l in range(depth):
        shift, scale, gate = (mod[l, :, j * d:(j + 1) * d].reshape(bsz, 1, d) for j in range(3))
        i = l // 2
        if l % 2 == 0:
            x = _layer0(x, shift, scale, gate, ln_g[l], w_in_ab[i], conv_w[i], sg_norm[i],
                        sg_w[i], sg_b[i], w_out_ab[i])
        else:
            q, k, vt, sz = _inproj1(x, shift, scale, ln_g[l], w_in_c[i], q_norm[i], k_norm[i])
            y = _attention(q, k, vt, sz)
            x = _outproj(y, x, gate, w_out_c[i])
    return x
```

```python
import functools

import jax
import jax.numpy as jnp
from jax import lax
from jax.experimental import pallas as pl
from jax.experimental.pallas import tpu as pltpu

F32 = jnp.float32
BF16 = jnp.bfloat16

EPS = 1e-6
CONV_W = 3
DH_B = 128
CHUNK = 128
DH_C = 64
LOG2E = 1.4426950408889634

LANES = 128
SUBLANES = 8
MXU_TILE = 256
VMEM_LIMIT_BYTES = 56 * 1024 * 1024

COL_BLOCK = MXU_TILE
ROW_TILE = 512
ATTN_BLOCK = 256


def _const_spec(shape):
    nd = len(shape)
    return pl.BlockSpec(shape, lambda *_: (0,) * nd, pipeline_mode=pl.Buffered(1))


def _dot(a, b):
    return jnp.dot(a, b, preferred_element_type=F32)


def _silu(x):
    return x * jax.nn.sigmoid(x)


def _mod_body(c_ref, w_ref, b_ref, o_ref):
    c = c_ref[...]
    o_ref[0] = jnp.dot(_silu(c), w_ref[0], preferred_element_type=F32,
                       precision=lax.Precision.HIGHEST) + b_ref[0]


def _modulation(c, ada_w, ada_b):
    depth, d, n3 = ada_w.shape
    bsz = c.shape[0]
    rows = -(-bsz // SUBLANES) * SUBLANES
    c_pad = jnp.pad(c, ((0, rows - bsz), (0, 0)))
    tn = d
    mod = pl.pallas_call(
        _mod_body,
        grid=(depth, n3 // tn),
        in_specs=[
            pl.BlockSpec((rows, d), lambda l, n: (0, 0)),
            pl.BlockSpec((1, d, tn), lambda l, n: (l, 0, n)),
            pl.BlockSpec((1, 1, tn), lambda l, n: (l, 0, n)),
        ],
        out_specs=pl.BlockSpec((1, rows, tn), lambda l, n: (l, 0, n)),
        out_shape=jax.ShapeDtypeStruct((depth, rows, n3), F32),
        compiler_params=pltpu.CompilerParams(
            dimension_semantics=("arbitrary", "arbitrary"),
            vmem_limit_bytes=VMEM_LIMIT_BYTES),
        name="adaln_modulation",
    )(c_pad, ada_w, ada_b.reshape(depth, 1, n3))
    return mod[:, :bsz, :]


def _modulated_norm(x, g, scale, shift):
    ms = jnp.mean(x * x, axis=-1, keepdims=True)
    xn = x * lax.rsqrt(ms + EPS)
    return (xn * g) * (1.0 + scale) + shift


def _layer0_body(x_ref, shift_ref, scale_ref, gate_ref, g_ref, win_ref, convw_ref,
                 sgn_ref, sgw_ref, sgb_ref, wout_ref, o_ref,
                 h_s, u_s, carry_s, y_s):
    tm, d = x_ref.shape
    d_a = convw_ref.shape[1]
    d_b = sgb_ref.shape[1]
    cw = COL_BLOCK

    @pl.when(pl.program_id(1) == 0)
    def _():
        carry_s[...] = jnp.zeros_like(carry_s)

    x = x_ref[...]
    h_s[...] = _modulated_norm(x, g_ref[...], scale_ref[...], shift_ref[...]).astype(BF16)
    hb = h_s[...]

    for cb in range(d_a // cw):
        c0 = cb * cw
        bg = _dot(hb, win_ref[:, c0:c0 + cw])
        cg = _dot(hb, win_ref[:, d_a + c0:d_a + c0 + cw])
        xa = _dot(hb, win_ref[:, 2 * d_a + c0:2 * d_a + c0 + cw])
        za = _dot(hb, win_ref[:, 3 * d_a + c0:3 * d_a + c0 + cw])
        u = cg * xa
        u_s[0:SUBLANES, :] = carry_s[:, c0:c0 + cw]
        u_s[SUBLANES:SUBLANES + tm, :] = u
        carry_s[:, c0:c0 + cw] = u[tm - SUBLANES:tm, :]
        u1 = u_s[SUBLANES - 1:SUBLANES - 1 + tm, :]
        u2 = u_s[SUBLANES - 2:SUBLANES - 2 + tm, :]
        conv = (convw_ref[2:3, c0:c0 + cw] * u + convw_ref[1:2, c0:c0 + cw] * u1
                + convw_ref[0:1, c0:c0 + cw] * u2)
        y_s[:, c0:c0 + cw] = (bg * conv * _silu(za)).astype(BF16)

    row = lax.broadcasted_iota(jnp.int32, (CHUNK, CHUNK), 0)
    col = lax.broadcasted_iota(jnp.int32, (CHUNK, CHUNK), 1)
    tril = row >= col
    off = 4 * d_a
    for cb in range(d_b // cw):
        c0 = cb * cw
        ub = _dot(hb, win_ref[:, off + c0:off + c0 + cw])
        vb = _dot(hb, win_ref[:, off + d_b + c0:off + d_b + c0 + cw])
        zb = _dot(hb, win_ref[:, off + 2 * d_b + c0:off + 2 * d_b + c0 + cw])
        gz = ub * _silu(zb)
        for gg in range(cw // DH_B):
            grp = (c0 + gg * DH_B) // DH_B
            l0 = gg * DH_B
            vg = vb[:, l0:l0 + DH_B]
            ms = jnp.mean(vg * vg, axis=-1, keepdims=True)
            vn = (vg * lax.rsqrt(ms + EPS) * sgn_ref[grp:grp + 1, :]).astype(BF16)
            wt = jnp.where(tril, sgw_ref[grp], 0.0).astype(BF16)
            bias = sgb_ref[:, c0 + l0:c0 + l0 + DH_B]
            for ch in range(tm // CHUNK):
                r0 = ch * CHUNK
                sgate = _dot(wt, vn[r0:r0 + CHUNK, :]) + bias
                y_s[r0:r0 + CHUNK, d_a + c0 + l0:d_a + c0 + l0 + DH_B] = (
                    gz[r0:r0 + CHUNK, l0:l0 + DH_B] * sgate).astype(BF16)

    out = _dot(y_s[...], wout_ref[...])
    o_ref[...] = x + gate_ref[...] * out


def _layer0(x, shift, scale, gate, ln_g, w_in, conv_w, sg_norm, sg_w, sg_b, w_out):
    bsz, s, d = x.shape
    d_a = conv_w.shape[1]
    g_b = sg_norm.shape[0]
    d_b = g_b * DH_B
    tm = min(ROW_TILE, s)
    assert s % tm == 0 and tm % CHUNK == 0 and d_a % COL_BLOCK == 0 and d_b % COL_BLOCK == 0
    bias_full = jnp.broadcast_to(sg_b.T[:, :, None], (CHUNK, g_b, DH_B)).reshape(CHUNK, d_b)
    vec = lambda: pl.BlockSpec((None, 1, d), lambda b, i: (b, 0, 0))
    return pl.pallas_call(
        _layer0_body,
        grid=(bsz, s // tm),
        in_specs=[
            pl.BlockSpec((None, tm, d), lambda b, i: (b, i, 0)),
            vec(), vec(), vec(),
            _const_spec((1, d)),
            _const_spec(w_in.shape),
            _const_spec(conv_w.shape),
            _const_spec(sg_norm.shape),
            _const_spec(sg_w.shape),
            _const_spec(bias_full.shape),
            _const_spec(w_out.shape),
        ],
        out_specs=pl.BlockSpec((None, tm, d), lambda b, i: (b, i, 0)),
        out_shape=jax.ShapeDtypeStruct((bsz, s, d), F32),
        scratch_shapes=[
            pltpu.VMEM((tm, d), BF16),
            pltpu.VMEM((tm + SUBLANES, COL_BLOCK), F32),
            pltpu.VMEM((SUBLANES, d_a), F32),
            pltpu.VMEM((tm, d_a + d_b), BF16),
        ],
        compiler_params=pltpu.CompilerParams(
            dimension_semantics=("arbitrary", "arbitrary"),
            vmem_limit_bytes=VMEM_LIMIT_BYTES),
        name="layer0_conv_sgmlp",
    )(x, shift, scale, gate, ln_g.reshape(1, d), w_in.astype(BF16), conv_w, sg_norm, sg_w,
      bias_full, w_out.astype(BF16))


def _inproj1_body(x_ref, shift_ref, scale_ref, g_ref, w_ref, qg_ref, kg_ref,
                  q_o, k_o, vt_o, sz_o, h_s, raw_s):
    tm, d = x_ref.shape
    cw = COL_BLOCK
    h_s[...] = _modulated_norm(x_ref[...], g_ref[...], scale_ref[...], shift_ref[...]).astype(BF16)
    hb = h_s[...]
    for cb in range(2 * d // cw):
        raw_s[:, cb * cw:(cb + 1) * cw] = _dot(hb, w_ref[:, cb * cw:(cb + 1) * cw])
    for cb in range(d // cw):
        c0 = cb * cw
        v = _dot(hb, w_ref[:, 2 * d + c0:2 * d + c0 + cw])
        for kb in range(tm // ATTN_BLOCK):
            for hp in range(cw // LANES):
                vt_o[c0 // LANES + hp, kb] = v[kb * ATTN_BLOCK:(kb + 1) * ATTN_BLOCK,
                                              hp * LANES:(hp + 1) * LANES].T.astype(BF16)
        sz_o[:, c0:c0 + cw] = _silu(_dot(hb, w_ref[:, 3 * d + c0:3 * d + c0 + cw])).astype(BF16)

    row = lax.broadcasted_iota(jnp.int32, (cw, cw), 0)
    col = lax.broadcasted_iota(jnp.int32, (cw, cw), 1)
    bd = jnp.where((row // DH_C) == (col // DH_C), 1.0, 0.0).astype(BF16)
    q_scale = LOG2E * (DH_C ** -0.5)
    for cb in range(2 * d // cw):
        c0 = cb * cw
        t = raw_s[:, c0:c0 + cw]
        ss = _dot((t * t).astype(BF16), bd)
        tn = t * lax.rsqrt(ss * (1.0 / DH_C) + EPS)
        if c0 < d:
            q_o[:, c0:c0 + cw] = (tn * qg_ref[:, c0:c0 + cw] * q_scale).astype(BF16)
        else:
            k_o[:, c0 - d:c0 - d + cw] = (tn * kg_ref[:, c0 - d:c0 - d + cw]).astype(BF16)


def _inproj1(x, shift, scale, ln_g, w_in, q_norm, k_norm):
    bsz, s, d = x.shape
    tm = min(ROW_TILE, s)
    assert s % tm == 0 and d % COL_BLOCK == 0 and COL_BLOCK % DH_C == 0
    assert tm % ATTN_BLOCK == 0 and COL_BLOCK % LANES == 0
    heads = d // DH_C
    vec = lambda: pl.BlockSpec((None, 1, d), lambda b, i: (b, 0, 0))
    tile = lambda: pl.BlockSpec((None, tm, d), lambda b, i: (b, i, 0))
    out = jax.ShapeDtypeStruct((bsz, s, d), BF16)
    return pl.pallas_call(
        _inproj1_body,
        grid=(bsz, s // tm),
        in_specs=[tile(), vec(), vec(), _const_spec((1, d)), _const_spec(w_in.shape),
                  _const_spec((1, d)), _const_spec((1, d))],
        out_specs=[tile(), tile(),
                   pl.BlockSpec((None, d // LANES, tm // ATTN_BLOCK, LANES, ATTN_BLOCK),
                                lambda b, i: (b, 0, i, 0, 0)),
                   tile()],
        out_shape=[out, out,
                   jax.ShapeDtypeStruct((bsz, d // LANES, s // ATTN_BLOCK, LANES, ATTN_BLOCK), BF16),
                   out],
        scratch_shapes=[pltpu.VMEM((tm, d), BF16), pltpu.VMEM((tm, 2 * d), F32)],
        compiler_params=pltpu.CompilerParams(
            dimension_semantics=("arbitrary", "arbitrary"),
            vmem_limit_bytes=VMEM_LIMIT_BYTES),
        name="layer1_qkvz_proj",
    )(x, shift, scale, ln_g.reshape(1, d), w_in.astype(BF16),
      jnp.tile(q_norm, heads).reshape(1, d), jnp.tile(k_norm, heads).reshape(1, d))


PIPE_DEPTH = 10
Z_MAX = 126.0
FINISH_UNROLL = 8
NEG_BIG = -1e30


def _attn_body(q_ref, k_ref, vt_ref, sz_ref, o_ref, qm_s, c_s, acc_s, zn_s, p_s, tot_s):
    s, width = q_ref.shape
    blk = ATTN_BLOCK
    nblk = s // blk
    heads = width // DH_C
    rows = heads * blk

    row = lax.broadcasted_iota(jnp.int32, (blk, blk), 0)
    col = lax.broadcasted_iota(jnp.int32, (blk, blk), 1)
    later = jnp.where(row > col, 1.0, 0.0).astype(BF16)
    causal = jnp.concatenate([col < row] * heads, axis=0)
    lane = lax.broadcasted_iota(jnp.int32, (blk, width), 1)

    def prep(i, _):
        q = q_ref[pl.ds(pl.multiple_of(i * blk, blk), blk), :]
        for hh in range(heads):
            keep = (lane >= hh * DH_C) & (lane < (hh + 1) * DH_C)
            qm_s[i, hh * blk:(hh + 1) * blk, :] = jnp.where(keep, q, jnp.zeros_like(q))
        return 0

    lax.fori_loop(0, nblk, prep, 0)

    def scores(idx, slot):
        kb = k_ref[pl.ds(pl.multiple_of(idx[1] * blk, blk), blk), :]
        z = lax.dot_general(qm_s[idx[0]], kb, (((1,), (1,)), ((), ())),
                            preferred_element_type=F32)
        zn_s[slot] = jnp.minimum(z, Z_MAX)

    def gate_logs(slot, diag):
        z = zn_s[slot]
        lm = jnp.log(1.0 + jnp.exp2(z)) * (-LOG2E)
        lb = lm + z
        if diag:
            lm = jnp.where(causal, lm, 0.0)
        suf = _dot(lm.astype(BF16), later)
        p = lb + suf
        if diag:
            p = jnp.where(causal, p, NEG_BIG)
        p_s[slot] = p
        tot_s[slot] = jnp.broadcast_to(suf[:, 0:1] + lm[:, 0:1], (rows, width))

    def weigh(idx, slot, first):
        if first:
            w = jnp.exp2(p_s[slot])
            c_s[idx[0]] = tot_s[slot]
        else:
            c = c_s[idx[0]]
            w = jnp.exp2(p_s[slot] + jnp.concatenate([c] * (blk // width), axis=1))
            c_s[idx[0]] = c + tot_s[slot]
        pv = lax.dot_general(vt_ref[idx[1]], w.astype(BF16), (((1,), (1,)), ((), ())),
                             preferred_element_type=F32)
        for hh in range(heads):
            part = pv[hh * DH_C:(hh + 1) * DH_C, hh * blk:(hh + 1) * blk]
            if first:
                acc_s[idx[0], hh * DH_C:(hh + 1) * DH_C, :] = part
            else:
                acc_s[idx[0], hh * DH_C:(hh + 1) * DH_C, :] += part

    def run_tiles(ntiles, first, step, diag):
        depth = PIPE_DEPTH
        if ntiles < 2 * depth:
            idx = first
            for _ in range(ntiles):
                scores(idx, 0)
                gate_logs(0, diag)
                weigh(idx, 0, diag)
                idx = step(idx)
            return

        def one_step(win, n, do_weigh=True, do_logs=True, do_scores=True):
            slot = n % depth
            nxt = step(win[-1])
            if do_weigh:
                weigh(win[0], slot, diag)
            if do_logs:
                gate_logs(slot, diag)
            if do_scores:
                scores(nxt, slot)
            return win[1:] + (nxt,)

        win = (first,) * (2 * depth)
        scores(first, 0)
        for n in range(1, 2 * depth):
            win = one_step(win, n, do_weigh=False, do_logs=n >= depth)
        steady = ntiles - 2 * depth
        n0 = 2 * depth
        for n in range(n0, n0 + steady % depth):
            win = one_step(win, n)
        n0 += steady % depth

        def body(_, win):
            for k in range(depth):
                win = one_step(win, n0 + k)
            return win

        win = lax.fori_loop(0, steady // depth, body, win)
        for n in range(ntiles, ntiles + 2 * depth):
            win = one_step(win, n, do_logs=n < ntiles + depth, do_scores=False)

    one = jnp.int32(1)
    zero = jnp.int32(0)
    run_tiles(nblk, (zero, zero), lambda ij: (ij[0] + 1, ij[1] + 1), True)
    run_tiles(nblk * (nblk - 1) // 2, (one, zero),
              lambda ij: (jnp.where(ij[1] > 0, ij[0], ij[0] + 1),
                          jnp.where(ij[1] > 0, ij[1] - 1, ij[0])), False)

    def finish(i, _):
        q0 = pl.multiple_of(i * blk, blk)
        o_ref[pl.ds(q0, blk), :] = (acc_s[i].T * sz_ref[pl.ds(q0, blk), :].astype(F32)).astype(BF16)
        return 0

    lax.fori_loop(0, nblk, finish, 0, unroll=FINISH_UNROLL if nblk % FINISH_UNROLL == 0 else 1)


def _attention(q, k, vt, sz):
    bsz, s, d = q.shape
    assert s % ATTN_BLOCK == 0 and d % LANES == 0 and LANES % DH_C == 0
    nblk = s // ATTN_BLOCK
    heads = LANES // DH_C
    spec = lambda: pl.BlockSpec((None, s, LANES), lambda b, h: (b, 0, h))
    return pl.pallas_call(
        _attn_body,
        grid=(bsz, d // LANES),
        in_specs=[spec(), spec(),
                  pl.BlockSpec((None, None, nblk, LANES, ATTN_BLOCK), lambda b, h: (b, h, 0, 0, 0)),
                  spec()],
        out_specs=spec(),
        out_shape=jax.ShapeDtypeStruct((bsz, s, d), BF16),
        scratch_shapes=[
            pltpu.VMEM((nblk, heads * ATTN_BLOCK, LANES), BF16),
            pltpu.VMEM((nblk, heads * ATTN_BLOCK, LANES), F32),
            pltpu.VMEM((nblk, LANES, ATTN_BLOCK), F32),
            pltpu.VMEM((PIPE_DEPTH, heads * ATTN_BLOCK, ATTN_BLOCK), F32),
            pltpu.VMEM((PIPE_DEPTH, heads * ATTN_BLOCK, ATTN_BLOCK), F32),
            pltpu.VMEM((PIPE_DEPTH, heads * ATTN_BLOCK, LANES), F32),
        ],
        compiler_params=pltpu.CompilerParams(
            dimension_semantics=("arbitrary", "arbitrary"),
            vmem_limit_bytes=VMEM_LIMIT_BYTES),
        name="stick_breaking_attention",
    )(q, k, vt, sz)


def _outproj_body(y_ref, x_ref, gate_ref, w_ref, o_ref):
    o_ref[...] = x_ref[...] + gate_ref[...] * _dot(y_ref[...], w_ref[...])


def _outproj(y, x, gate, w_out):
    bsz, s, d = x.shape
    tm = min(ROW_TILE, s)
    assert s % tm == 0
    return pl.pallas_call(
        _outproj_body,
        grid=(bsz, s // tm),
        in_specs=[
            pl.BlockSpec((None, tm, y.shape[2]), lambda b, i: (b, i, 0)),
            pl.BlockSpec((None, tm, d), lambda b, i: (b, i, 0)),
            pl.BlockSpec((None, 1, d), lambda b, i: (b, 0, 0)),
            _const_spec(w_out.shape),
        ],
        out_specs=pl.BlockSpec((None, tm, d), lambda b, i: (b, i, 0)),
        out_shape=jax.ShapeDtypeStruct((bsz, s, d), F32),
        compiler_params=pltpu.CompilerParams(
            dimension_semantics=("arbitrary", "arbitrary"),
            vmem_limit_bytes=VMEM_LIMIT_BYTES),
        name="layer1_out_proj",
    )(y, x, gate, w_out.astype(BF16))


def kernel(x, c, ln_g, ada_w, ada_b, w_in_ab, conv_w, sg_norm, sg_w, sg_b, w_out_ab,
           w_in_c, q_norm, k_norm, w_out_c):
    bsz, _, d = x.shape
    depth = ada_w.shape[0]
    mod = _modulation(c, ada_w, ada_b)
    for l in range(depth):
        shift, scale, gate = (mod[l, :, j * d:(j + 1) * d].reshape(bsz, 1, d) for j in range(3))
        i = l // 2
        if l % 2 == 0:
            x = _layer0(x, shift, scale, gate, ln_g[l], w_in_ab[i], conv_w[i], sg_norm[i],
                        sg_w[i], sg_b[i], w_out_ab[i])
        else:
            q, k, vt, sz = _inproj1(x, shift, scale, ln_g[l], w_in_c[i], q_norm[i], k_norm[i])
            y = _attention(q, k, vt, sz)
            x = _outproj(y, x, gate, w_out_c[i])
    return x
```

```python
import functools

import jax
import jax.numpy as jnp
from jax import lax
from jax.experimental import pallas as pl
from jax.experimental.pallas import tpu as pltpu

F32 = jnp.float32
BF16 = jnp.bfloat16

EPS = 1e-6
CONV_W = 3
DH_B = 128
CHUNK = 128
DH_C = 64
LOG2E = 1.4426950408889634

LANES = 128
SUBLANES = 8
MXU_TILE = 256
VMEM_LIMIT_BYTES = 56 * 1024 * 1024

COL_BLOCK = MXU_TILE
ROW_TILE = 512
ATTN_BLOCK = 256


def _const_spec(shape):
    nd = len(shape)
    return pl.BlockSpec(shape, lambda *_: (0,) * nd, pipeline_mode=pl.Buffered(1))


def _dot(a, b):
    return jnp.dot(a, b, preferred_element_type=F32)


def _silu(x):
    return x * jax.nn.sigmoid(x)


def _mod_body(c_ref, w_ref, b_ref, o_ref):
    c = c_ref[...]
    o_ref[0] = jnp.dot(_silu(c), w_ref[0], preferred_element_type=F32,
                       precision=lax.Precision.HIGHEST) + b_ref[0]


def _modulation(c, ada_w, ada_b):
    depth, d, n3 = ada_w.shape
    bsz = c.shape[0]
    rows = -(-bsz // SUBLANES) * SUBLANES
    c_pad = jnp.pad(c, ((0, rows - bsz), (0, 0)))
    tn = d
    mod = pl.pallas_call(
        _mod_body,
        grid=(depth, n3 // tn),
        in_specs=[
            pl.BlockSpec((rows, d), lambda l, n: (0, 0)),
            pl.BlockSpec((1, d, tn), lambda l, n: (l, 0, n)),
            pl.BlockSpec((1, 1, tn), lambda l, n: (l, 0, n)),
        ],
        out_specs=pl.BlockSpec((1, rows, tn), lambda l, n: (l, 0, n)),
        out_shape=jax.ShapeDtypeStruct((depth, rows, n3), F32),
        compiler_params=pltpu.CompilerParams(
            dimension_semantics=("arbitrary", "arbitrary"),
            vmem_limit_bytes=VMEM_LIMIT_BYTES),
        name="adaln_modulation",
    )(c_pad, ada_w, ada_b.reshape(depth, 1, n3))
    return mod[:, :bsz, :]


def _modulated_norm(x, g, scale, shift):
    ms = jnp.mean(x * x, axis=-1, keepdims=True)
    xn = x * lax.rsqrt(ms + EPS)
    return (xn * g) * (1.0 + scale) + shift


def _layer0_body(x_ref, shift_ref, scale_ref, gate_ref, g_ref, win_ref, convw_ref,
                 sgn_ref, sgw_ref, sgb_ref, wout_ref, o_ref,
                 h_s, u_s, carry_s, y_s):
    tm, d = x_ref.shape
    d_a = convw_ref.shape[1]
    d_b = sgb_ref.shape[1]
    cw = COL_BLOCK

    @pl.when(pl.program_id(1) == 0)
    def _():
        carry_s[...] = jnp.zeros_like(carry_s)

    x = x_ref[...]
    h_s[...] = _modulated_norm(x, g_ref[...], scale_ref[...], shift_ref[...]).astype(BF16)
    hb = h_s[...]

    for cb in range(d_a // cw):
        c0 = cb * cw
        bg = _dot(hb, win_ref[:, c0:c0 + cw])
        cg = _dot(hb, win_ref[:, d_a + c0:d_a + c0 + cw])
        xa = _dot(hb, win_ref[:, 2 * d_a + c0:2 * d_a + c0 + cw])
        za = _dot(hb, win_ref[:, 3 * d_a + c0:3 * d_a + c0 + cw])
        u = cg * xa
        u_s[0:SUBLANES, :] = carry_s[:, c0:c0 + cw]
        u_s[SUBLANES:SUBLANES + tm, :] = u
        carry_s[:, c0:c0 + cw] = u[tm - SUBLANES:tm, :]
        u1 = u_s[SUBLANES - 1:SUBLANES - 1 + tm, :]
        u2 = u_s[SUBLANES - 2:SUBLANES - 2 + tm, :]
        conv = (convw_ref[2:3, c0:c0 + cw] * u + convw_ref[1:2, c0:c0 + cw] * u1
                + convw_ref[0:1, c0:c0 + cw] * u2)
        y_s[:, c0:c0 + cw] = (bg * conv * _silu(za)).astype(BF16)

    row = lax.broadcasted_iota(jnp.int32, (CHUNK, CHUNK), 0)
    col = lax.broadcasted_iota(jnp.int32, (CHUNK, CHUNK), 1)
    tril = row >= col
    off = 4 * d_a
    for cb in range(d_b // cw):
        c0 = cb * cw
        ub = _dot(hb, win_ref[:, off + c0:off + c0 + cw])
        vb = _dot(hb, win_ref[:, off + d_b + c0:off + d_b + c0 + cw])
        zb = _dot(hb, win_ref[:, off + 2 * d_b + c0:off + 2 * d_b + c0 + cw])
        gz = ub * _silu(zb)
        for gg in range(cw // DH_B):
            grp = (c0 + gg * DH_B) // DH_B
            l0 = gg * DH_B
            vg = vb[:, l0:l0 + DH_B]
            ms = jnp.mean(vg * vg, axis=-1, keepdims=True)
            vn = (vg * lax.rsqrt(ms + EPS) * sgn_ref[grp:grp + 1, :]).astype(BF16)
            wt = jnp.where(tril, sgw_ref[grp], 0.0).astype(BF16)
            bias = sgb_ref[:, c0 + l0:c0 + l0 + DH_B]
            for ch in range(tm // CHUNK):
                r0 = ch * CHUNK
                sgate = _dot(wt, vn[r0:r0 + CHUNK, :]) + bias
                y_s[r0:r0 + CHUNK, d_a + c0 + l0:d_a + c0 + l0 + DH_B] = (
                    gz[r0:r0 + CHUNK, l0:l0 + DH_B] * sgate).astype(BF16)

    out = _dot(y_s[...], wout_ref[...])
    o_ref[...] = x + gate_ref[...] * out


def _layer0(x, shift, scale, gate, ln_g, w_in, conv_w, sg_norm, sg_w, sg_b, w_out):
    bsz, s, d = x.shape
    d_a = conv_w.shape[1]
    g_b = sg_norm.shape[0]
    d_b = g_b * DH_B
    tm = min(ROW_TILE, s)
    assert s % tm == 0 and tm % CHUNK == 0 and d_a % COL_BLOCK == 0 and d_b % COL_BLOCK == 0
    bias_full = jnp.broadcast_to(sg_b.T[:, :, None], (CHUNK, g_b, DH_B)).reshape(CHUNK, d_b)
    vec = lambda: pl.BlockSpec((None, 1, d), lambda b, i: (b, 0, 0))
    return pl.pallas_call(
        _layer0_body,
        grid=(bsz, s // tm),
        in_specs=[
            pl.BlockSpec((None, tm, d), lambda b, i: (b, i, 0)),
            vec(), vec(), vec(),
            _const_spec((1, d)),
            _const_spec(w_in.shape),
            _const_spec(conv_w.shape),
            _const_spec(sg_norm.shape),
            _const_spec(sg_w.shape),
            _const_spec(bias_full.shape),
            _const_spec(w_out.shape),
        ],
        out_specs=pl.BlockSpec((None, tm, d), lambda b, i: (b, i, 0)),
        out_shape=jax.ShapeDtypeStruct((bsz, s, d), F32),
        scratch_shapes=[
            pltpu.VMEM((tm, d), BF16),
            pltpu.VMEM((tm + SUBLANES, COL_BLOCK), F32),
            pltpu.VMEM((SUBLANES, d_a), F32),
            pltpu.VMEM((tm, d_a + d_b), BF16),
        ],
        compiler_params=pltpu.CompilerParams(
            dimension_semantics=("arbitrary", "arbitrary"),
            vmem_limit_bytes=VMEM_LIMIT_BYTES),
        name="layer0_conv_sgmlp",
    )(x, shift, scale, gate, ln_g.reshape(1, d), w_in.astype(BF16), conv_w, sg_norm, sg_w,
      bias_full, w_out.astype(BF16))


def _inproj1_body(x_ref, shift_ref, scale_ref, g_ref, w_ref, qg_ref, kg_ref,
                  q_o, k_o, vt_o, sz_o, h_s, raw_s):
    tm, d = x_ref.shape
    cw = COL_BLOCK
    h_s[...] = _modulated_norm(x_ref[...], g_ref[...], scale_ref[...], shift_ref[...]).astype(BF16)
    hb = h_s[...]
    for cb in range(2 * d // cw):
        raw_s[:, cb * cw:(cb + 1) * cw] = _dot(hb, w_ref[:, cb * cw:(cb + 1) * cw])
    for cb in range(d // cw):
        c0 = cb * cw
        v = _dot(hb, w_ref[:, 2 * d + c0:2 * d + c0 + cw])
        for kb in range(tm // ATTN_BLOCK):
            for hp in range(cw // LANES):
                vt_o[c0 // LANES + hp, kb] = v[kb * ATTN_BLOCK:(kb + 1) * ATTN_BLOCK,
                                              hp * LANES:(hp + 1) * LANES].T.astype(BF16)
        sz_o[:, c0:c0 + cw] = _silu(_dot(hb, w_ref[:, 3 * d + c0:3 * d + c0 + cw])).astype(BF16)

    row = lax.broadcasted_iota(jnp.int32, (cw, cw), 0)
    col = lax.broadcasted_iota(jnp.int32, (cw, cw), 1)
    bd = jnp.where((row // DH_C) == (col // DH_C), 1.0, 0.0).astype(BF16)
    q_scale = LOG2E * (DH_C ** -0.5)
    for cb in range(2 * d // cw):
        c0 = cb * cw
        t = raw_s[:, c0:c0 + cw]
        ss = _dot((t * t).astype(BF16), bd)
        tn = t * lax.rsqrt(ss * (1.0 / DH_C) + EPS)
        if c0 < d:
            q_o[:, c0:c0 + cw] = (tn * qg_ref[:, c0:c0 + cw] * q_scale).astype(BF16)
        else:
            k_o[:, c0 - d:c0 - d + cw] = (tn * kg_ref[:, c0 - d:c0 - d + cw]).astype(BF16)


def _inproj1(x, shift, scale, ln_g, w_in, q_norm, k_norm):
    bsz, s, d = x.shape
    tm = min(ROW_TILE, s)
    assert s % tm == 0 and d % COL_BLOCK == 0 and COL_BLOCK % DH_C == 0
    assert tm % ATTN_BLOCK == 0 and COL_BLOCK % LANES == 0
    heads = d // DH_C
    vec = lambda: pl.BlockSpec((None, 1, d), lambda b, i: (b, 0, 0))
    tile = lambda: pl.BlockSpec((None, tm, d), lambda b, i: (b, i, 0))
    out = jax.ShapeDtypeStruct((bsz, s, d), BF16)
    return pl.pallas_call(
        _inproj1_body,
        grid=(bsz, s // tm),
        in_specs=[tile(), vec(), vec(), _const_spec((1, d)), _const_spec(w_in.shape),
                  _const_spec((1, d)), _const_spec((1, d))],
        out_specs=[tile(), tile(),
                   pl.BlockSpec((None, d // LANES, tm // ATTN_BLOCK, LANES, ATTN_BLOCK),
                                lambda b, i: (b, 0, i, 0, 0)),
                   tile()],
        out_shape=[out, out,
                   jax.ShapeDtypeStruct((bsz, d // LANES, s // ATTN_BLOCK, LANES, ATTN_BLOCK), BF16),
                   out],
        scratch_shapes=[pltpu.VMEM((tm, d), BF16), pltpu.VMEM((tm, 2 * d), F32)],
        compiler_params=pltpu.CompilerParams(
            dimension_semantics=("arbitrary", "arbitrary"),
            vmem_limit_bytes=VMEM_LIMIT_BYTES),
        name="layer1_qkvz_proj",
    )(x, shift, scale, ln_g.reshape(1, d), w_in.astype(BF16),
      jnp.tile(q_norm, heads).reshape(1, d), jnp.tile(k_norm, heads).reshape(1, d))


PIPE_DEPTH = 10
Z_MAX = 126.0
FINISH_UNROLL = 8
NEG_BIG = -1e30


def _attn_body(q_ref, k_ref, vt_ref, sz_ref, o_ref, qm_s, c_s, acc_s, zn_s, p_s, tot_s):
    s, width = q_ref.shape
    blk = ATTN_BLOCK
    nblk = s // blk
    heads = width // DH_C
    rows = heads * blk
    sub = SUBLANES
    extra = 2 * sub

    row = lax.broadcasted_iota(jnp.int32, (blk, blk), 0)
    col = lax.broadcasted_iota(jnp.int32, (blk, blk), 1)
    later = jnp.concatenate([jnp.where(col > row, 1.0, 0.0), jnp.ones((extra, blk), F32)],
                            axis=0).astype(BF16)
    causal = jnp.concatenate([row < col] * heads, axis=1)
    dim = lax.broadcasted_iota(jnp.int32, (width, blk), 0)

    def prep(i, _):
        qt = q_ref[pl.ds(pl.multiple_of(i * blk, blk), blk), :].astype(F32).T
        qm_s[i] = jnp.concatenate(
            [jnp.where((dim >= hh * DH_C) & (dim < (hh + 1) * DH_C), qt, 0.0) for hh in range(heads)],
            axis=1).astype(BF16)
        return 0

    lax.fori_loop(0, nblk, prep, 0)

    def scores(idx, slot):
        kb = k_ref[pl.ds(pl.multiple_of(idx[1] * blk, blk), blk), :]
        z = _dot(kb, qm_s[idx[0]])
        zn_s[slot] = jnp.minimum(z, Z_MAX)

    def gate_logs(slot, diag):
        z = zn_s[slot]
        lm = jnp.log(1.0 + jnp.exp2(z)) * (-LOG2E)
        lb = lm + z
        if diag:
            lm = jnp.where(causal, lm, 0.0)
        suf = _dot(later, lm.astype(BF16))
        p = lb + suf[0:blk, :]
        if diag:
            p = jnp.where(causal, p, NEG_BIG)
        p_s[slot] = p
        tot_s[slot] = suf[blk:blk + sub, :]

    def weigh(idx, slot, first):
        if first:
            w = jnp.exp2(p_s[slot])
            c_s[idx[0]] = tot_s[slot]
        else:
            c = c_s[idx[0]]
            w = jnp.exp2(p_s[slot].reshape(blk // sub, sub, rows) + c[None]).reshape(blk, rows)
            c_s[idx[0]] = c + tot_s[slot]
        pv = _dot(vt_ref[idx[1]], w.astype(BF16))
        for hh in range(heads):
            part = pv[hh * DH_C:(hh + 1) * DH_C, hh * blk:(hh + 1) * blk]
            if first:
                acc_s[idx[0], hh * DH_C:(hh + 1) * DH_C, :] = part
            else:
                acc_s[idx[0], hh * DH_C:(hh + 1) * DH_C, :] += part

    def run_tiles(ntiles, first, step, diag):
        depth = PIPE_DEPTH
        if ntiles < 2 * depth:
            idx = first
            for _ in range(ntiles):
                scores(idx, 0)
                gate_logs(0, diag)
                weigh(idx, 0, diag)
                idx = step(idx)
            return

        def one_step(win, n, do_weigh=True, do_logs=True, do_scores=True):
            slot = n % depth
            nxt = step(win[-1])
            if do_weigh:
                weigh(win[0], slot, diag)
            if do_logs:
                gate_logs(slot, diag)
            if do_scores:
                scores(nxt, slot)
            return win[1:] + (nxt,)

        win = (first,) * (2 * depth)
        scores(first, 0)
        for n in range(1, 2 * depth):
            win = one_step(win, n, do_weigh=False, do_logs=n >= depth)
        steady = ntiles - 2 * depth
        n0 = 2 * depth
        for n in range(n0, n0 + steady % depth):
            win = one_step(win, n)
        n0 += steady % depth

        def body(_, win):
            for k in range(depth):
                win = one_step(win, n0 + k)
            return win

        win = lax.fori_loop(0, steady // depth, body, win)
        for n in range(ntiles, ntiles + 2 * depth):
            win = one_step(win, n, do_logs=n < ntiles + depth, do_scores=False)

    one = jnp.int32(1)
    zero = jnp.int32(0)
    run_tiles(nblk, (zero, zero), lambda ij: (ij[0] + 1, ij[1] + 1), True)
    run_tiles(nblk * (nblk - 1) // 2, (one, zero),
              lambda ij: (jnp.where(ij[1] > 0, ij[0], ij[0] + 1),
                          jnp.where(ij[1] > 0, ij[1] - 1, ij[0])), False)

    def finish(i, _):
        q0 = pl.multiple_of(i * blk, blk)
        o_ref[pl.ds(q0, blk), :] = (acc_s[i].T * sz_ref[pl.ds(q0, blk), :].astype(F32)).astype(BF16)
        return 0

    lax.fori_loop(0, nblk, finish, 0, unroll=FINISH_UNROLL if nblk % FINISH_UNROLL == 0 else 1)


def _attention(q, k, vt, sz):
    bsz, s, d = q.shape
    assert s % ATTN_BLOCK == 0 and d % LANES == 0 and LANES % DH_C == 0
    nblk = s // ATTN_BLOCK
    heads = LANES // DH_C
    spec = lambda: pl.BlockSpec((None, s, LANES), lambda b, h: (b, 0, h))
    return pl.pallas_call(
        _attn_body,
        grid=(bsz, d // LANES),
        in_specs=[spec(), spec(),
                  pl.BlockSpec((None, None, nblk, LANES, ATTN_BLOCK), lambda b, h: (b, h, 0, 0, 0)),
                  spec()],
        out_specs=spec(),
        out_shape=jax.ShapeDtypeStruct((bsz, s, d), BF16),
        scratch_shapes=[
            pltpu.VMEM((nblk, LANES, heads * ATTN_BLOCK), BF16),
            pltpu.VMEM((nblk, SUBLANES, heads * ATTN_BLOCK), F32),
            pltpu.VMEM((nblk, LANES, ATTN_BLOCK), F32),
            pltpu.VMEM((PIPE_DEPTH, ATTN_BLOCK, heads * ATTN_BLOCK), F32),
            pltpu.VMEM((PIPE_DEPTH, ATTN_BLOCK, heads * ATTN_BLOCK), F32),
            pltpu.VMEM((PIPE_DEPTH, SUBLANES, heads * ATTN_BLOCK), F32),
        ],
        compiler_params=pltpu.CompilerParams(
            dimension_semantics=("arbitrary", "arbitrary"),
            vmem_limit_bytes=VMEM_LIMIT_BYTES),
        name="stick_breaking_attention",
    )(q, k, vt, sz)


def _outproj_body(y_ref, x_ref, gate_ref, w_ref, o_ref):
    o_ref[...] = x_ref[...] + gate_ref[...] * _dot(y_ref[...], w_ref[...])


def _outproj(y, x, gate, w_out):
    bsz, s, d = x.shape
    tm = min(ROW_TILE, s)
    assert s % tm == 0
    return pl.pallas_call(
        _outproj_body,
        grid=(bsz, s // tm),
        in_specs=[
            pl.BlockSpec((None, tm, y.shape[2]), lambda b, i: (b, i, 0)),
            pl.BlockSpec((None, tm, d), lambda b, i: (b, i, 0)),
            pl.BlockSpec((None, 1, d), lambda b, i: (b, 0, 0)),
            _const_spec(w_out.shape),
        ],
        out_specs=pl.BlockSpec((None, tm, d), lambda b, i: (b, i, 0)),
        out_shape=jax.ShapeDtypeStruct((bsz, s, d), F32),
        compiler_params=pltpu.CompilerParams(
            dimension_semantics=("arbitrary", "arbitrary"),
            vmem_limit_bytes=VMEM_LIMIT_BYTES),
        name="layer1_out_proj",
    )(y, x, gate, w_out.astype(BF16))


def kernel(x, c, ln_g, ada_w, ada_b, w_in_ab, conv_w, sg_norm, sg_w, sg_b, w_out_ab,
           w_in_c, q_norm, k_norm, w_out_c):
    bsz, _, d = x.shape
    depth = ada_w.shape[0]
    mod = _modulation(c, ada_w, ada_b)
    for l in range(depth):
        shift, scale, gate = (mod[l, :, j * d:(j + 1) * d].reshape(bsz, 1, d) for j in range(3))
        i = l // 2
        if l % 2 == 0:
            x = _layer0(x, shift, scale, gate, ln_g[l], w_in_ab[i], conv_w[i], sg_norm[i],
                        sg_w[i], sg_b[i], w_out_ab[i])
        else:
            q, k, vt, sz = _inproj1(x, shift, scale, ln_g[l], w_in_c[i], q_norm[i], k_norm[i])
            y = _attention(q, k, vt, sz)
            x = _outproj(y, x, gate, w_out_c[i])
    return x
```
